```python
import jax, jax.numpy as jnp
from jax import lax
import numpy as np

D_MODEL = 1024
BATCH = 2
SEQ = 8192
DEPTH = 4
DEC_BATCH = 128
DEC_SEQ = 8
PAST_LEN = 8192
PAGE_SIZE = 128

HEAD_DIM = 64
A_HEADS = 16
A_KV_HEADS = 4
A_GROUP = A_HEADS // A_KV_HEADS
A_WINDOW = 128
A_Q_W = A_HEADS * HEAD_DIM
A_KV_W = A_KV_HEADS * HEAD_DIM
A_QKV_WIDTH = A_Q_W + 2 * A_KV_W
B_SLOTS = 8
B_PATTERNS = ((128, 1), (512, 4), (2048, 16))
B_N_GROUPS = len(B_PATTERNS)
B_HEADS = B_N_GROUPS * B_SLOTS
B_MAX_WINDOW = max(w for w, _ in B_PATTERNS)
D_FF = -(-8 * D_MODEL // (3 * 256)) * 256
N_A_LAYERS = DEPTH // 2
N_B_LAYERS = DEPTH - N_A_LAYERS
BLOCK = 128
EPS = 1e-6
NEG = -1e30

kernel_name = 'yoco_swa_sink_dilated_decode_step'


def rms_norm(x, g):
    xf = x.astype(jnp.float32)
    y = xf * lax.rsqrt(jnp.mean(xf * xf, axis=-1, keepdims=True) + EPS)
    return y.astype(x.dtype) * g


def alibi_slopes(n):
    return jnp.asarray(2.0 ** (-8.0 * (np.arange(n) + 1) / n), dtype=jnp.float32)


def swiglu(x, w_gu, w_dn):
    gate, up = jnp.split(x @ w_gu, 2, axis=-1)
    return (jax.nn.silu(gate) * up) @ w_dn


def qkv_a(h, w_qkv):
    n, t = h.shape[:2]
    q, k, v = jnp.split(h @ w_qkv, [A_Q_W, A_Q_W + A_KV_W], axis=-1)
    return (q.reshape(n, t, A_KV_HEADS, A_GROUP, HEAD_DIM),
            k.reshape(n, t, A_KV_HEADS, HEAD_DIM),
            v.reshape(n, t, A_KV_HEADS, HEAD_DIM))


def shared_kv(x, g_kv, w_kv_s):
    n, t = x.shape[:2]
    k, v = jnp.split(rms_norm(x, g_kv) @ w_kv_s, 2, axis=-1)
    return k.reshape(n, t, B_SLOTS, HEAD_DIM), v.reshape(n, t, B_SLOTS, HEAD_DIM)


def banded_attention(q, k, v, slopes, dist_unit, max_dist, sinks=None):
    n, L, kh, g, d = q.shape
    nb = -(-L // BLOCK)
    pad = nb * BLOCK - L
    q = jnp.pad(q, ((0, 0), (0, pad), (0, 0), (0, 0), (0, 0)))
    k = jnp.pad(k, ((0, 0), (BLOCK, pad), (0, 0), (0, 0)))
    v = jnp.pad(v, ((0, 0), (BLOCK, pad), (0, 0), (0, 0)))
    qb = q.reshape(n, nb, BLOCK, kh, g, d)
    kb = k.reshape(n, nb + 1, BLOCK, kh, d)
    vb = v.reshape(n, nb + 1, BLOCK, kh, d)
    kc = jnp.concatenate([kb[:, :-1], kb[:, 1:]], axis=2)
    vc = jnp.concatenate([vb[:, :-1], vb[:, 1:]], axis=2)
    s = jnp.einsum('nbqhgd,nbkhd->nbhgqk', qb, kc).astype(jnp.float32) * HEAD_DIM ** -0.5
    qi = jnp.arange(BLOCK)[:, None]
    kj = jnp.arange(2 * BLOCK)[None, :]
    dist = BLOCK + qi - kj
    kpos = jnp.arange(nb)[:, None, None] * BLOCK + kj[None] - BLOCK
    valid = (dist >= 0) & (dist <= max_dist) & (kpos >= 0)
    s = s - slopes[None, None, :, :, None, None] * (dist_unit * dist).astype(jnp.float32)
    s = jnp.where(valid[None, :, None, None], s, NEG)
    lse = jax.nn.logsumexp(s, axis=-1)
    if sinks is not None:
        lse = jnp.logaddexp(lse, sinks.astype(jnp.float32)[None, None, :, :, None])
    p = jnp.exp(s - lse[..., None])
    o = jnp.einsum('nbhgqk,nbkhd->nbqhgd', p.astype(v.dtype), vc)
    o = o.reshape(n, nb * BLOCK, kh, g, d)[:, :L]
    lse = jnp.moveaxis(lse, -1, 2).reshape(n, nb * BLOCK, kh, g)[:, :L]
    return o, lse


def window_attention_sample(q, kc, vc, slopes, sinks):
    S = q.shape[1]
    W = kc.shape[1] - S
    s = jnp.einsum('bqhgd,bkhd->bhgqk', q, kc).astype(jnp.float32) * HEAD_DIM ** -0.5
    dist = (W + jnp.arange(S))[:, None] - jnp.arange(W + S)[None, :]
    valid = (dist >= 0) & (dist < A_WINDOW)
    s = s - slopes[None, :, :, None, None] * dist.astype(jnp.float32)
    s = jnp.where(valid, s, NEG)
    lse = jnp.logaddexp(jax.nn.logsumexp(s, axis=-1), sinks.astype(jnp.float32)[None, :, :, None])
    p = jnp.exp(s - lse[..., None])
    return jnp.einsum('bhgqk,bkhd->bqhgd', p.astype(vc.dtype), vc)


def to_sub(x, r):
    n, t = x.shape[:2]
    rest = x.shape[2:]
    x = jnp.swapaxes(x.reshape((n, t // r, r) + rest), 1, 2)
    return x.reshape((n * r, t // r) + rest)


def from_sub(x, n, r):
    L = x.shape[1]
    rest = x.shape[2:]
    x = jnp.swapaxes(x.reshape((n, r, L) + rest), 1, 2)
    return x.reshape((n, L * r) + rest)


def combine_groups(outs, lses):
    wts = jax.nn.softmax(jnp.stack(lses, axis=0), axis=0)
    return jnp.einsum('gnts,gntsd->ntsd', wts.astype(outs[0].dtype), jnp.stack(outs, axis=0))


def dilated_prompt(q, k, v, slopes):
    n = q.shape[0]
    outs, lses = [], []
    for gi, (w, r) in enumerate(B_PATTERNS):
        qs = to_sub(q[:, :, gi], r)[:, :, :, None]
        o, lse = banded_attention(qs, to_sub(k, r), to_sub(v, r), slopes[gi][:, None], r, w // r)
        outs.append(from_sub(o[:, :, :, 0], n, r))
        lses.append(from_sub(lse[..., 0], n, r))
    return combine_groups(outs, lses)


def dilated_sample(q, kc, vc, slopes):
    S = q.shape[1]
    W = kc.shape[1] - S
    outs, lses = [], []
    for gi, (w, r) in enumerate(B_PATTERNS):
        steps = jnp.arange(w // r + 1)
        idx = W + jnp.arange(S)[:, None] - r * steps[None, :]
        valid = idx >= 0
        idx = jnp.maximum(idx, 0)
        kg = kc[:, idx]
        vg = vc[:, idx]
        s = jnp.einsum('bshd,bskhd->bhsk', q[:, :, gi], kg).astype(jnp.float32) * HEAD_DIM ** -0.5
        s = s - slopes[gi][None, :, None, None] * (r * steps).astype(jnp.float32)
        s = jnp.where(valid[None, None], s, NEG)
        lse = jax.nn.logsumexp(s, axis=-1)
        p = jnp.exp(s - lse[..., None])
        outs.append(jnp.einsum('bhsk,bskhd->bshd', p.astype(vc.dtype), vg))
        lses.append(jnp.swapaxes(lse, 1, 2))
    return combine_groups(outs, lses)


def setup_inputs(seed: int = 0) -> dict:
    key = jax.random.key(seed)
    ks = jax.random.split(key, 18)
    wa = min(A_WINDOW, PAST_LEN)
    wb = min(B_MAX_WINDOW, PAST_LEN)

    def nrm(k, shape, scale=1.0):
        return scale * jax.random.normal(k, shape, jnp.float32)

    return {
        'x_prompt': nrm(ks[0], (BATCH, SEQ, D_MODEL)),
        'x_sample': nrm(ks[1], (DEC_BATCH, DEC_SEQ, D_MODEL)),
        'cache_a_k': nrm(ks[2], (N_A_LAYERS, DEC_BATCH, wa, A_KV_HEADS, HEAD_DIM)),
        'cache_a_v': nrm(ks[3], (N_A_LAYERS, DEC_BATCH, wa, A_KV_HEADS, HEAD_DIM)),
        'cache_b_k': nrm(ks[4], (DEC_BATCH, wb, B_SLOTS, HEAD_DIM)),
        'cache_b_v': nrm(ks[5], (DEC_BATCH, wb, B_SLOTS, HEAD_DIM)),
        'g_attn': 1.0 + nrm(ks[6], (DEPTH, D_MODEL), 0.05),
        'g_ffn': 1.0 + nrm(ks[7], (DEPTH, D_MODEL), 0.05),
        'w_qkv_a': nrm(ks[8], (N_A_LAYERS, D_MODEL, A_QKV_WIDTH), D_MODEL ** -0.5),
        'sinks_a': nrm(ks[9], (N_A_LAYERS, A_HEADS), 0.5),
        'w_o_a': nrm(ks[10], (N_A_LAYERS, A_Q_W, D_MODEL), A_Q_W ** -0.5),
        'g_kv': 1.0 + nrm(ks[11], (D_MODEL,), 0.05),
        'w_kv_s': nrm(ks[12], (D_MODEL, 2 * B_SLOTS * HEAD_DIM), D_MODEL ** -0.5),
        'w_q_b': nrm(ks[13], (N_B_LAYERS, D_MODEL, B_HEADS * HEAD_DIM), D_MODEL ** -0.5),
        'w_o_b': nrm(ks[14], (N_B_LAYERS, B_SLOTS * HEAD_DIM, D_MODEL), (B_SLOTS * HEAD_DIM) ** -0.5),
        'w_gate_up': nrm(ks[15], (DEPTH, D_MODEL, 2 * D_FF), D_MODEL ** -0.5),
        'w_down': nrm(ks[16], (DEPTH, D_FF, D_MODEL), D_FF ** -0.5),
        'g_final': 1.0 + nrm(ks[17], (D_MODEL,), 0.05),
    }


def reference(x_prompt, x_sample, cache_a_k, cache_a_v, cache_b_k, cache_b_v,
              g_attn, g_ffn, w_qkv_a, sinks_a, w_o_a, g_kv, w_kv_s, w_q_b, w_o_b,
              w_gate_up, w_down, g_final):
    slopes_a = alibi_slopes(A_HEADS).reshape(A_KV_HEADS, A_GROUP)
    slopes_b = alibi_slopes(B_HEADS).reshape(B_N_GROUPS, B_SLOTS)
    wa_p = min(A_WINDOW, x_prompt.shape[1])
    wb_p = min(B_MAX_WINDOW, x_prompt.shape[1])
    wa = cache_a_k.shape[2]
    wb = cache_b_k.shape[1]
    xp, xs = x_prompt, x_sample
    bp, tp = xp.shape[:2]
    bs, ts = xs.shape[:2]
    a_k_p, a_v_p, a_k_s, a_v_s = [], [], [], []
    for l in range(DEPTH):
        if l < N_A_LAYERS:
            hp = rms_norm(xp, g_attn[l])
            hs = rms_norm(xs, g_attn[l])
            qp, kp, vp = qkv_a(hp, w_qkv_a[l])
            qs, ks_, vs_ = qkv_a(hs, w_qkv_a[l])
            sink = sinks_a[l].reshape(A_KV_HEADS, A_GROUP)
            op, _ = banded_attention(qp, kp, vp, slopes_a, 1, A_WINDOW - 1, sink)
            kcs = jnp.concatenate([cache_a_k[l], ks_], axis=1)
            vcs = jnp.concatenate([cache_a_v[l], vs_], axis=1)
            osm = window_attention_sample(qs, kcs, vcs, slopes_a, sink)
            xp = xp + op.reshape(bp, tp, A_Q_W) @ w_o_a[l]
            xs = xs + osm.reshape(bs, ts, A_Q_W) @ w_o_a[l]
            a_k_p.append(kp[:, -wa_p:])
            a_v_p.append(vp[:, -wa_p:])
            a_k_s.append(kcs[:, -wa:])
            a_v_s.append(vcs[:, -wa:])
        else:
            if l == N_A_LAYERS:
                kbp, vbp = shared_kv(xp, g_kv, w_kv_s)
                kbs, vbs = shared_kv(xs, g_kv, w_kv_s)
                kcb = jnp.concatenate([cache_b_k, kbs], axis=1)
                vcb = jnp.concatenate([cache_b_v, vbs], axis=1)
            b = l - N_A_LAYERS
            hp = rms_norm(xp, g_attn[l])
            hs = rms_norm(xs, g_attn[l])
            qp = (hp @ w_q_b[b]).reshape(bp, tp, B_N_GROUPS, B_SLOTS, HEAD_DIM)
            qs = (hs @ w_q_b[b]).reshape(bs, ts, B_N_GROUPS, B_SLOTS, HEAD_DIM)
            op = dilated_prompt(qp, kbp, vbp, slopes_b)
            osm = dilated_sample(qs, kcb, vcb, slopes_b)
            xp = xp + op.reshape(bp, tp, B_SLOTS * HEAD_DIM) @ w_o_b[b]
            xs = xs + osm.reshape(bs, ts, B_SLOTS * HEAD_DIM) @ w_o_b[b]
        xp = xp + swiglu(rms_norm(xp, g_ffn[l]), w_gate_up[l], w_down[l])
        xs = xs + swiglu(rms_norm(xs, g_ffn[l]), w_gate_up[l], w_down[l])
    y_prompt = rms_norm(xp, g_final)
    y_sample = rms_norm(xs, g_final)
    return (y_prompt, y_sample,
            jnp.stack(a_k_p, axis=0), jnp.stack(a_v_p, axis=0), kbp[:, -wb_p:], vbp[:, -wb_p:],
            jnp.stack(a_k_s, axis=0), jnp.stack(a_v_s, axis=0), kcb[:, -wb:], vcb[:, -wb:])
```

```python
import functools

import numpy as np
import jax
import jax.numpy as jnp
from jax import lax
from jax.experimental import pallas as pl
from jax.experimental.pallas import tpu as pltpu

D_MODEL = 1024
HEAD_DIM = 64
A_HEADS = 16
A_KV_HEADS = 4
A_GROUP = A_HEADS // A_KV_HEADS
A_WINDOW = 128
A_Q_W = A_HEADS * HEAD_DIM
A_KV_W = A_KV_HEADS * HEAD_DIM
B_SLOTS = 8
B_PATTERNS = ((128, 1), (512, 4), (2048, 16))
B_N_GROUPS = len(B_PATTERNS)
B_HEADS = B_N_GROUPS * B_SLOTS
B_W = B_SLOTS * HEAD_DIM
BLOCK = 128
EPS = 1e-6
NEG = -1e30
SCALE = HEAD_DIM ** -0.5

LANES = 128
HALF = LANES // 2
VMEM_LIMIT = 56 * 1024 * 1024

F32 = jnp.float32
BF16 = jnp.bfloat16


def _slopes(n):
    return [float(2.0 ** (-8.0 * (i + 1) / n)) for i in range(n)]


SLOPES_A = _slopes(A_HEADS)
SLOPES_B = _slopes(B_HEADS)


def _rms(x, g):
    ms = jnp.mean(x * x, axis=-1, keepdims=True)
    return (x * lax.rsqrt(ms + EPS)) * g


def _params(n_grid):
    return pltpu.CompilerParams(dimension_semantics=("arbitrary",) * n_grid,
                                vmem_limit_bytes=VMEM_LIMIT)


def _const_spec(shape):
    nd = len(shape)
    return pl.BlockSpec(shape, lambda *_: (0,) * nd, pipeline_mode=pl.Buffered(1))


def _norm_mm_kernel(x_ref, g_ref, w_ref, *out_refs, splits):
    h = _rms(x_ref[...], g_ref[...])
    y = jnp.dot(h.astype(BF16), w_ref[...], preferred_element_type=F32)
    off = 0
    for o_ref, width in zip(out_refs, splits):
        o_ref[...] = y[:, off:off + width].astype(o_ref.dtype)
        off += width


def norm_mm(x, g, w, splits, dtypes, tm=512):
    rows, d = x.shape
    n = w.shape[1]
    assert rows % tm == 0 and sum(splits) == n
    return pl.pallas_call(
        functools.partial(_norm_mm_kernel, splits=tuple(splits)),
        grid=(rows // tm,),
        in_specs=[pl.BlockSpec((tm, d), lambda i: (i, 0)),
                  _const_spec((1, d)),
                  _const_spec((d, n))],
        out_specs=[pl.BlockSpec((tm, s), lambda i: (i, 0)) for s in splits],
        out_shape=[jax.ShapeDtypeStruct((rows, s), dt) for s, dt in zip(splits, dtypes)],
        compiler_params=_params(1),
        name="norm_mm",
    )(x, g.reshape(1, d), w)


def _proj_ffn_kernel(*refs, combine, final, d_ff):
    it = iter(refs)
    x_ref = next(it)
    if combine:
        o_refs = [next(it) for _ in range(B_N_GROUPS)]
        l_refs = [next(it) for _ in range(B_N_GROUPS)]
    else:
        o_ref = next(it)
    wo_ref, g_ref, wgu_ref, wdn_ref = next(it), next(it), next(it), next(it)
    gfin_ref = next(it) if final else None
    out_ref = next(it)

    if combine:
        ls = [r[...] for r in l_refs]
        mx = jnp.maximum(jnp.maximum(ls[0], ls[1]), ls[2])
        es = [jnp.exp(l - mx) for l in ls]
        den = es[0] + es[1] + es[2]
        num = es[0] * o_refs[0][...].astype(F32)
        num = num + es[1] * o_refs[1][...].astype(F32)
        num = num + es[2] * o_refs[2][...].astype(F32)
        o = (num / den).astype(BF16)
    else:
        o = o_ref[...]
    x = x_ref[...] + jnp.dot(o, wo_ref[...], preferred_element_type=F32)
    h = _rms(x, g_ref[...]).astype(BF16)
    gu = jnp.dot(h, wgu_ref[...], preferred_element_type=F32)
    gate = gu[:, :d_ff]
    up = gu[:, d_ff:]
    act = (gate / (1.0 + jnp.exp(-gate))) * up
    y = x + jnp.dot(act.astype(BF16), wdn_ref[...], preferred_element_type=F32)
    if final:
        y = _rms(y, gfin_ref[...])
    out_ref[...] = y


def proj_ffn(x, o, lses, w_o, g, w_gu, w_dn, g_final=None, tm=256):
    rows, d = x.shape
    combine = lses is not None
    final = g_final is not None
    d_ff = w_dn.shape[0]
    ow = w_o.shape[0]
    row_spec = lambda w: pl.BlockSpec((tm, w), lambda i: (i, 0))
    args, in_specs = [x], [row_spec(d)]
    if combine:
        args += list(o) + list(lses)
        in_specs += [row_spec(ow)] * (2 * B_N_GROUPS)
    else:
        args.append(o)
        in_specs.append(row_spec(ow))
    args += [w_o, g.reshape(1, d), w_gu, w_dn]
    in_specs += [_const_spec(w_o.shape), _const_spec((1, d)), _const_spec(w_gu.shape),
                 _const_spec(w_dn.shape)]
    if final:
        args.append(g_final.reshape(1, d))
        in_specs.append(_const_spec((1, d)))
    return pl.pallas_call(
        functools.partial(_proj_ffn_kernel, combine=combine, final=final, d_ff=d_ff),
        grid=(rows // tm,),
        in_specs=in_specs,
        out_specs=row_spec(d),
        out_shape=jax.ShapeDtypeStruct((rows, d), F32),
        compiler_params=_params(1),
        name="proj_ffn",
    )(*args)


def _band_attn_kernel(*refs, units, n_kv_pairs, has_sink, emit_lse, dist_unit, max_dist):
    it = iter(refs)
    sink_ref = next(it) if has_sink else None
    q_ref, kc_ref, kp_ref, vc_ref, vp_ref = (next(it) for _ in range(5))
    o_ref = next(it)
    lse_ref = next(it) if emit_lse else None
    bias_ref = next(it)

    first = (pl.program_id(0) == 0) & (pl.program_id(1) == 0)

    @pl.when(first)
    def _():
        qi = lax.broadcasted_iota(jnp.int32, (2 * BLOCK, 2 * BLOCK), 0) & (BLOCK - 1)
        kj = lax.broadcasted_iota(jnp.int32, (2 * BLOCK, 2 * BLOCK), 1)
        row = lax.broadcasted_iota(jnp.int32, (2 * BLOCK, 2 * BLOCK), 0)
        dist = BLOCK + qi - kj
        valid = (dist >= 0) & (dist <= max_dist)
        distf = (dist_unit * dist).astype(F32)
        for u, (_, _, slope_lo, slope_hi, _, _) in enumerate(units):
            slope = jnp.where(row < BLOCK, slope_lo, slope_hi)
            b = -(slope * distf)
            bias_ref[0, u] = jnp.where(valid, b, NEG)
            bias_ref[1, u] = jnp.where(valid & (kj >= BLOCK), b, NEG)

    sel = jnp.where(pl.program_id(1) == 0, 1, 0)
    lane = lax.broadcasted_iota(jnp.int32, (BLOCK, LANES), 1)
    lo = lane < HALF
    row1 = lax.broadcasted_iota(jnp.int32, (2 * BLOCK, 1), 0)
    ones = jnp.ones((2 * BLOCK, LANES), BF16)

    kcat, vcat = [], []
    for p in range(n_kv_pairs):
        sl = slice(p * LANES, (p + 1) * LANES)
        kcat.append(jnp.concatenate([kp_ref[:, sl], kc_ref[:, sl]], axis=0).astype(BF16))
        vcat.append(jnp.concatenate([vp_ref[:, sl], vc_ref[:, sl]], axis=0).astype(BF16))

    for u, (qoff, p, _, _, sink_lo, sink_hi) in enumerate(units):
        qs = q_ref[:, qoff:qoff + LANES].astype(F32) * SCALE
        q2 = jnp.concatenate([jnp.where(lo, qs, 0.0), jnp.where(lo, 0.0, qs)], axis=0).astype(BF16)
        s = lax.dot_general(q2, kcat[p], (((1,), (1,)), ((), ())), preferred_element_type=F32)
        t = s + bias_ref[sel, u]
        m = jnp.max(t, axis=1, keepdims=True)
        e = jnp.exp(t - m).astype(BF16)
        pv = jnp.dot(e, vcat[p], preferred_element_type=F32)
        l = jnp.dot(e, ones, preferred_element_type=F32)
        if has_sink:
            sink = jnp.where(row1 < BLOCK, sink_ref[sink_lo], sink_ref[sink_hi])
            l = l + jnp.exp(sink - m)
        o2 = pv / l
        o_ref[:, qoff:qoff + LANES] = jnp.where(lo, o2[:BLOCK], o2[BLOCK:]).astype(o_ref.dtype)
        if emit_lse:
            lse2 = m + jnp.log(l)
            lse_ref[:, qoff:qoff + LANES] = jnp.where(lo, lse2[:BLOCK], lse2[BLOCK:])


def band_attn(q, k, v, sinks, units, dist_unit, max_dist, emit_lse):
    n, L, qw = q.shape
    kw = k.shape[2]
    nb = L // BLOCK
    has_sink = sinks is not None
    cur = lambda w: pl.BlockSpec((None, BLOCK, w), lambda b, i: (b, i, 0))
    prev = lambda w: pl.BlockSpec((None, BLOCK, w), lambda b, i: (b, jnp.maximum(i - 1, 0), 0))
    args, in_specs = [], []
    if has_sink:
        args.append(sinks)
        in_specs.append(pl.BlockSpec(memory_space=pltpu.SMEM))
    args += [q, k, k, v, v]
    in_specs += [cur(qw), cur(kw), prev(kw), cur(kw), prev(kw)]
    out_shape = [jax.ShapeDtypeStruct((n, L, qw), BF16)]
    out_specs = [cur(qw)]
    if emit_lse:
        out_shape.append(jax.ShapeDtypeStruct((n, L, qw), F32))
        out_specs.append(cur(qw))
    res = pl.pallas_call(
        functools.partial(_band_attn_kernel, units=tuple(units), n_kv_pairs=kw // LANES,
                          has_sink=has_sink, emit_lse=emit_lse, dist_unit=dist_unit,
                          max_dist=max_dist),
        grid=(n, nb),
        in_specs=in_specs,
        out_specs=out_specs,
        out_shape=out_shape,
        scratch_shapes=[pltpu.VMEM((2, len(units), 2 * BLOCK, 2 * BLOCK), F32)],
        compiler_params=_params(2),
        name="band_attn",
    )(*args)
    return res if emit_lse else res[0]


def _units_a():
    units = []
    for g in range(A_GROUP):
        for p in range(A_KV_HEADS // 2):
            h_lo = (2 * p) * A_GROUP + g
            h_hi = (2 * p + 1) * A_GROUP + g
            units.append((g * A_KV_W + p * LANES, p, SLOPES_A[h_lo], SLOPES_A[h_hi], h_lo, h_hi))
    return units


def _units_b(gi):
    units = []
    for p in range(B_SLOTS // 2):
        units.append((p * LANES, p, SLOPES_B[gi * B_SLOTS + 2 * p], SLOPES_B[gi * B_SLOTS + 2 * p + 1],
                      0, 0))
    return units


def _sample_a_kernel(sink_ref, q_ref, kt_ref, vt_ref, knt_ref, vnt_ref, o_ref, ko_ref, vo_ref,
                     *, bb, n_new):
    w = kt_ref.shape[-1]
    rows = A_GROUP * n_new
    assert n_new & (n_new - 1) == 0
    s_idx = lax.broadcasted_iota(jnp.int32, (rows, w), 0) & (n_new - 1)
    row1 = lax.broadcasted_iota(jnp.int32, (rows, 1), 0)
    j = lax.broadcasted_iota(jnp.int32, (rows, w), 1)
    dist_c = w + s_idx - j
    valid_c = (dist_c >= 0) & (dist_c < A_WINDOW)
    dist_n = s_idx - (j - (w - n_new))
    valid_n = (j >= w - n_new) & (dist_n >= 0) & (dist_n < A_WINDOW)
    lane_new = lax.broadcasted_iota(jnp.int32, (HEAD_DIM, w), 1) >= w - n_new

    for kh in range(A_KV_HEADS):
        slope = jnp.zeros((rows, 1), F32)
        sink = jnp.zeros((rows, 1), F32)
        for g in range(A_GROUP):
            slope = jnp.where(row1 >= g * n_new, SLOPES_A[kh * A_GROUP + g], slope)
            sink = jnp.where(row1 >= g * n_new, sink_ref[kh * A_GROUP + g], sink)
        bias_c = jnp.where(valid_c, -(slope * dist_c.astype(F32)), NEG)
        bias_n = jnp.where(valid_n, -(slope * dist_n.astype(F32)), NEG)
        for b in range(bb):
            kt = kt_ref[b, kh]
            vt = vt_ref[b, kh]
            knt = knt_ref[b, kh]
            vnt = vnt_ref[b, kh]
            q = (q_ref[b, kh].astype(F32) * SCALE).astype(BF16)
            tc = jnp.dot(q, kt.astype(BF16), preferred_element_type=F32) + bias_c
            tn = jnp.dot(q, knt.astype(BF16), preferred_element_type=F32) + bias_n
            m = jnp.maximum(jnp.max(tc, axis=1, keepdims=True), jnp.max(tn, axis=1, keepdims=True))
            ec = jnp.exp(tc - m)
            en = jnp.exp(tn - m)
            l = (jnp.sum(ec, axis=1, keepdims=True) + jnp.sum(en, axis=1, keepdims=True)
                 + jnp.exp(sink - m))
            inv = 1.0 / l
            pc = (ec * inv).astype(BF16)
            pn = (en * inv).astype(BF16)
            nt = (((1,), (1,)), ((), ()))
            ot = (lax.dot_general(vt.astype(BF16), pc, nt, preferred_element_type=F32)
                  + lax.dot_general(vnt.astype(BF16), pn, nt, preferred_element_type=F32))
            o_ref[b, kh] = ot
            ko_ref[b, kh] = jnp.where(lane_new, knt, pltpu.roll(kt, w - n_new, 1))
            vo_ref[b, kh] = jnp.where(lane_new, vnt, pltpu.roll(vt, w - n_new, 1))


def sample_attn_a(sinks, q, kt, vt, knt, vnt, bb=2):
    n, kh, d, w = kt.shape
    rows = q.shape[2]
    n_new = rows // A_GROUP
    spec = lambda a, b_: pl.BlockSpec((bb, kh, a, b_), lambda i: (i, 0, 0, 0))
    return pl.pallas_call(
        functools.partial(_sample_a_kernel, bb=bb, n_new=n_new),
        grid=(n // bb,),
        in_specs=[pl.BlockSpec(memory_space=pltpu.SMEM), spec(rows, d), spec(d, w), spec(d, w),
                  spec(d, w), spec(d, w)],
        out_specs=[spec(d, rows), spec(d, w), spec(d, w)],
        out_shape=[jax.ShapeDtypeStruct((n, kh, d, rows), F32),
                   jax.ShapeDtypeStruct((n, kh, d, w), F32),
                   jax.ShapeDtypeStruct((n, kh, d, w), F32)],
        compiler_params=_params(1),
        name="sample_attn_a",
    )(sinks, q, kt, vt, knt, vnt)


def _sample_b_kernel(*refs, n_new, emit_cache):
    if emit_cache:
        q_ref, kt_ref, vt_ref, knt_ref, vnt_ref, o_ref, ko_ref, vo_ref = refs
    else:
        q_ref, kt_ref, vt_ref, knt_ref, vnt_ref, o_ref = refs
    w = kt_ref.shape[-1]
    s_c = lax.broadcasted_iota(jnp.int32, (n_new, w), 0)
    j_c = lax.broadcasted_iota(jnp.int32, (n_new, w), 1)
    dist_c = w + s_c - j_c
    s_n = lax.broadcasted_iota(jnp.int32, (n_new, LANES), 0)
    j_n = lax.broadcasted_iota(jnp.int32, (n_new, LANES), 1)
    dist_n = s_n - (j_n - (LANES - n_new))
    negd_c, negd_n, starts = [], [], []
    for (win, r) in B_PATTERNS:
        start = max(w - (-(-win // LANES)) * LANES, 0)
        starts.append(start)
        assert r & (r - 1) == 0
        vc = (dist_c >= 0) & (dist_c <= win) & ((dist_c & (r - 1)) == 0)
        vn = (j_n >= LANES - n_new) & (dist_n >= 0) & (dist_n <= win) & ((dist_n & (r - 1)) == 0)
        negd_c.append(jnp.where(vc, -dist_c.astype(F32), NEG)[:, start:])
        negd_n.append(jnp.where(vn, -dist_n.astype(F32), NEG))
    lane_new = lax.broadcasted_iota(jnp.int32, (HEAD_DIM, LANES), 1) >= LANES - n_new
    nt = (((1,), (1,)), ((), ()))

    for slot in range(B_SLOTS):
        kt = kt_ref[0, slot]
        vt = vt_ref[0, slot]
        knt = knt_ref[0, slot]
        vnt = vnt_ref[0, slot]
        q = (q_ref[0, slot].astype(F32) * SCALE).astype(BF16)
        sc = jnp.dot(q, kt.astype(BF16), preferred_element_type=F32)
        sn = jnp.dot(q, knt.astype(BF16), preferred_element_type=F32)
        es, ens, ls, lses = [], [], [], []
        for gi in range(B_N_GROUPS):
            slope = SLOPES_B[gi * B_SLOTS + slot]
            rs = slice(gi * n_new, (gi + 1) * n_new)
            tc = sc[rs, starts[gi]:] + negd_c[gi] * slope
            tn = sn[rs] + negd_n[gi] * slope
            m = jnp.maximum(jnp.max(tc, axis=1, keepdims=True), jnp.max(tn, axis=1, keepdims=True))
            ec = jnp.exp(tc - m)
            en = jnp.exp(tn - m)
            l = jnp.sum(ec, axis=1, keepdims=True) + jnp.sum(en, axis=1, keepdims=True)
            es.append(ec)
            ens.append(en)
            ls.append(l)
            lses.append(m + jnp.log(l))
        mx = jnp.maximum(jnp.maximum(lses[0], lses[1]), lses[2])
        ws = [jnp.exp(x - mx) for x in lses]
        den = ws[0] + ws[1] + ws[2]
        cs = [ws[gi] / (den * ls[gi]) for gi in range(B_N_GROUPS)]
        tiles = []
        for tix in range(w // LANES):
            acc = None
            for gi in range(B_N_GROUPS):
                off = tix * LANES - starts[gi]
                if off >= 0:
                    term = cs[gi] * es[gi][:, off:off + LANES]
                    acc = term if acc is None else acc + term
            tiles.append(acc)
        pc = jnp.concatenate(tiles, axis=1).astype(BF16)
        pn = (cs[0] * ens[0] + cs[1] * ens[1] + cs[2] * ens[2]).astype(BF16)
        ot = (lax.dot_general(vt.astype(BF16), pc, nt, preferred_element_type=F32)
              + lax.dot_general(vnt.astype(BF16), pn, nt, preferred_element_type=F32))
        o_ref[0, slot] = ot
        if emit_cache:
            for src, new, dst in ((kt, knt, ko_ref), (vt, vnt, vo_ref)):
                rolled = pltpu.roll(src, w - n_new, 1)
                dst[0, slot, :, :w - LANES] = rolled[:, :w - LANES]
                dst[0, slot, :, w - LANES:] = jnp.where(lane_new, new, rolled[:, w - LANES:])


def sample_attn_b(q, kt, vt, knt, vnt, n_new, emit_cache):
    n, slots, d, w = kt.shape
    rows = q.shape[2]
    spec = lambda a, b_: pl.BlockSpec((1, slots, a, b_), lambda i: (i, 0, 0, 0))
    out_specs = [spec(d, n_new)]
    out_shape = [jax.ShapeDtypeStruct((n, slots, d, n_new), F32)]
    if emit_cache:
        out_specs += [spec(d, w), spec(d, w)]
        out_shape += [jax.ShapeDtypeStruct((n, slots, d, w), F32)] * 2
    return pl.pallas_call(
        functools.partial(_sample_b_kernel, n_new=n_new, emit_cache=emit_cache),
        grid=(n,),
        in_specs=[spec(rows, d), spec(d, w), spec(d, w), spec(d, LANES), spec(d, LANES)],
        out_specs=out_specs,
        out_shape=out_shape,
        compiler_params=_params(1),
        name="sample_attn_b",
    )(q, kt, vt, knt, vnt)


def _to_sub(x, r):
    n, t, c = x.shape
    return jnp.swapaxes(x.reshape(n, t // r, r, c), 1, 2).reshape(n * r, t // r, c)


def _from_sub(x, n, r):
    _, L, c = x.shape
    return jnp.swapaxes(x.reshape(n, r, L, c), 1, 2).reshape(n, L * r, c)


def _new_rows_t(x, n, s, heads):
    xt = x.reshape(n, s, heads, HEAD_DIM).transpose(0, 2, 3, 1)
    return jnp.pad(xt, ((0, 0), (0, 0), (0, 0), (LANES - s, 0)))


def kernel(x_prompt, x_sample, cache_a_k, cache_a_v, cache_b_k, cache_b_v, g_attn, g_ffn, w_qkv_a,
           sinks_a, w_o_a, g_kv, w_kv_s, w_q_b, w_o_b, w_gate_up, w_down, g_final):
    bp, tp, d = x_prompt.shape
    bs, ts, _ = x_sample.shape
    n_a = w_qkv_a.shape[0]
    depth = g_attn.shape[0]
    wa = cache_a_k.shape[2]
    wb = cache_b_k.shape[1]
    wa_p = min(A_WINDOW, tp)
    wb_p = min(max(w for w, _ in B_PATTERNS), tp)
    assert wa == LANES and wb % LANES == 0 and tp % (BLOCK * 16) == 0

    xp = x_prompt.reshape(bp * tp, d)
    xs = x_sample.reshape(bs * ts, d)

    wq = w_qkv_a[:, :, :A_Q_W].reshape(n_a, d, A_KV_HEADS, A_GROUP, HEAD_DIM)
    wq = wq.transpose(0, 1, 3, 2, 4).reshape(n_a, d, A_Q_W)
    w_qkv = jnp.concatenate([wq, w_qkv_a[:, :, A_Q_W:]], axis=2).astype(BF16)
    w_oa = w_o_a.reshape(n_a, A_KV_HEADS, A_GROUP, HEAD_DIM, d).transpose(0, 2, 1, 3, 4)
    w_oa = w_oa.reshape(n_a, A_Q_W, d).astype(BF16)
    w_kv = w_kv_s.astype(BF16)
    w_qb = w_q_b.astype(BF16)
    w_ob = w_o_b.astype(BF16)
    w_gu = w_gate_up.astype(BF16)
    w_dn = w_down.astype(BF16)

    cak_t = cache_a_k.transpose(0, 1, 3, 4, 2)
    cav_t = cache_a_v.transpose(0, 1, 3, 4, 2)
    cbk_t = cache_b_k.transpose(0, 2, 3, 1)
    cbv_t = cache_b_v.transpose(0, 2, 3, 1)

    units_a = _units_a()
    a_k_p, a_v_p, a_k_s, a_v_s = [], [], [], []
    qkv_splits = (A_Q_W, A_KV_W, A_KV_W)
    qkv_dtypes = (BF16, F32, F32)

    for l in range(depth):
        last = l == depth - 1
        gfin = g_final if last else None
        if l < n_a:
            qp, kp, vp = norm_mm(xp, g_attn[l], w_qkv[l], qkv_splits, qkv_dtypes)
            qs, ks_, vs_ = norm_mm(xs, g_attn[l], w_qkv[l], qkv_splits, qkv_dtypes)
            op = band_attn(qp.reshape(bp, tp, A_Q_W), kp.reshape(bp, tp, A_KV_W),
                           vp.reshape(bp, tp, A_KV_W), sinks_a[l], units_a, 1, A_WINDOW - 1, False)
            qs_t = qs.reshape(bs, ts, A_GROUP, A_KV_HEADS, HEAD_DIM).transpose(0, 3, 2, 1, 4)
            qs_t = qs_t.reshape(bs, A_KV_HEADS, A_GROUP * ts, HEAD_DIM)
            knt = _new_rows_t(ks_, bs, ts, A_KV_HEADS)
            vnt = _new_rows_t(vs_, bs, ts, A_KV_HEADS)
            os_t, ko, vo = sample_attn_a(sinks_a[l], qs_t, cak_t[l], cav_t[l], knt, vnt)
            osm = os_t.reshape(bs, A_KV_HEADS, HEAD_DIM, A_GROUP, ts).transpose(0, 4, 3, 1, 2)
            osm = osm.reshape(bs * ts, A_Q_W).astype(BF16)
            a_k_p.append(kp.reshape(bp, tp, A_KV_HEADS, HEAD_DIM)[:, -wa_p:])
            a_v_p.append(vp.reshape(bp, tp, A_KV_HEADS, HEAD_DIM)[:, -wa_p:])
            a_k_s.append(ko.transpose(0, 3, 1, 2))
            a_v_s.append(vo.transpose(0, 3, 1, 2))
            xp = proj_ffn(xp, op.reshape(bp * tp, A_Q_W), None, w_oa[l], g_ffn[l], w_gu[l], w_dn[l], gfin)
            xs = proj_ffn(xs, osm, None, w_oa[l], g_ffn[l], w_gu[l], w_dn[l], gfin)
        else:
            b = l - n_a
            if l == n_a:
                kbp, vbp = norm_mm(xp, g_kv, w_kv, (B_W, B_W), (F32, F32))
                kbs, vbs = norm_mm(xs, g_kv, w_kv, (B_W, B_W), (F32, F32))
                kb16 = kbp.astype(BF16).reshape(bp, tp, B_W)
                vb16 = vbp.astype(BF16).reshape(bp, tp, B_W)
                k_sub = [_to_sub(kb16, r) for _, r in B_PATTERNS]
                v_sub = [_to_sub(vb16, r) for _, r in B_PATTERNS]
                kbnt = _new_rows_t(kbs, bs, ts, B_SLOTS)
                vbnt = _new_rows_t(vbs, bs, ts, B_SLOTS)
            q_groups = norm_mm(xp, g_attn[l], w_qb[b], (B_W,) * B_N_GROUPS, (BF16,) * B_N_GROUPS)
            outs, lses = [], []
            for gi, (w, r) in enumerate(B_PATTERNS):
                q_sub = _to_sub(q_groups[gi].reshape(bp, tp, B_W), r)
                o_g, lse_g = band_attn(q_sub, k_sub[gi], v_sub[gi], None, _units_b(gi), r, w // r, True)
                outs.append(_from_sub(o_g, bp, r).reshape(bp * tp, B_W))
                lses.append(_from_sub(lse_g, bp, r).reshape(bp * tp, B_W))
            qs_groups = norm_mm(xs, g_attn[l], w_qb[b], (B_W,) * B_N_GROUPS, (BF16,) * B_N_GROUPS)
            qs_t = jnp.stack([q.reshape(bs, ts, B_SLOTS, HEAD_DIM) for q in qs_groups], axis=1)
            qs_t = qs_t.transpose(0, 3, 1, 2, 4).reshape(bs, B_SLOTS, B_N_GROUPS * ts, HEAD_DIM)
            qs_t = jnp.pad(qs_t, ((0, 0), (0, 0), (0, -(B_N_GROUPS * ts) % 16), (0, 0)))
            if l == n_a:
                os_t, kcb_t, vcb_t = sample_attn_b(qs_t, cbk_t, cbv_t, kbnt, vbnt, ts, True)
            else:
                os_t = sample_attn_b(qs_t, cbk_t, cbv_t, kbnt, vbnt, ts, False)[0]
            osm = os_t.transpose(0, 3, 1, 2).reshape(bs * ts, B_W).astype(BF16)
            xp = proj_ffn(xp, outs, lses, w_ob[b], g_ffn[l], w_gu[l], w_dn[l], gfin)
            xs = proj_ffn(xs, osm, None, w_ob[b], g_ffn[l], w_gu[l], w_dn[l], gfin)

    y_prompt = xp.reshape(bp, tp, d)
    y_sample = xs.reshape(bs, ts, d)
    b_k_p = kbp.reshape(bp, tp, B_SLOTS, HEAD_DIM)[:, -wb_p:]
    b_v_p = vbp.reshape(bp, tp, B_SLOTS, HEAD_DIM)[:, -wb_p:]
    return (y_prompt, y_sample,
            jnp.stack(a_k_p, axis=0), jnp.stack(a_v_p, axis=0), b_k_p, b_v_p,
            jnp.stack(a_k_s, axis=0), jnp.stack(a_v_s, axis=0),
            kcb_t.transpose(0, 3, 1, 2), vcb_t.transpose(0, 3, 1, 2))
```

```python
import functools
import math

import jax
import jax.numpy as jnp
from jax import lax
from jax.experimental import pallas as pl
from jax.experimental.pallas import tpu as pltpu

D_MODEL = 1024
HEAD_DIM = 64
A_HEADS = 16
A_KV_HEADS = 4
A_GROUP = A_HEADS // A_KV_HEADS
A_WINDOW = 128
A_Q_W = A_HEADS * HEAD_DIM
A_KV_W = A_KV_HEADS * HEAD_DIM
B_SLOTS = 8
B_PATTERNS = ((128, 1), (512, 4), (2048, 16))
B_N_GROUPS = len(B_PATTERNS)
B_HEADS = B_N_GROUPS * B_SLOTS
B_W = B_SLOTS * HEAD_DIM
BLOCK = 128
EPS = 1e-6
NEG = -1e30
LOG2E = math.log2(math.e)
LN2 = math.log(2.0)
QSCALE = HEAD_DIM ** -0.5 * LOG2E

LANES = 128
HALF = LANES // 2
SUPER = 16 * BLOCK
VMEM_LIMIT = 56 * 1024 * 1024

F32 = jnp.float32
BF16 = jnp.bfloat16
NT_DIMS = (((1,), (1,)), ((), ()))


def _slopes(n):
    return [float(2.0 ** (-8.0 * (i + 1) / n)) for i in range(n)]


SLOPES_A = _slopes(A_HEADS)
SLOPES_B = _slopes(B_HEADS)


def _rms(x, g):
    ms = jnp.mean(x * x, axis=-1, keepdims=True)
    return (x * lax.rsqrt(ms + EPS)) * g


def _params(n_grid):
    return pltpu.CompilerParams(dimension_semantics=("arbitrary",) * n_grid,
                                vmem_limit_bytes=VMEM_LIMIT)


def _const_spec(shape):
    nd = len(shape)
    return pl.BlockSpec(shape, lambda *_: (0,) * nd, pipeline_mode=pl.Buffered(1))


def _layer_spec(w, l):
    return pl.BlockSpec((None,) + w.shape[1:], lambda *_: (l, 0, 0), pipeline_mode=pl.Buffered(1))


def _norm_mm_kernel(x_ref, g_ref, w_ref, *refs, outs):
    out_refs, scr_refs = refs[:len(outs)], refs[len(outs):]
    h = _rms(x_ref[...], g_ref[...])
    y = jnp.dot(h.astype(BF16), w_ref[...], preferred_element_type=F32)
    si = 0
    for o_ref, (off, width, r, _) in zip(out_refs, outs):
        yo = y[:, off:off + width]
        if r == 1:
            o_ref[...] = yo.astype(o_ref.dtype)
        else:
            scr = scr_refs[si]
            si += 1
            rows = yo.shape[0] // r
            for j in range(width // LANES):
                cols = slice(j * LANES, (j + 1) * LANES)
                scr[j] = yo[:, cols]
                for c in range(r):
                    o_ref[c, :, cols] = scr[j, pl.ds(c, rows, stride=r), :].astype(o_ref.dtype)


def norm_mm(x, g, w, l, outs, tm=512):
    bsz, t, d = x.shape
    assert t % tm == 0
    in_specs = [pl.BlockSpec((None, tm, d), lambda b, j: (b, j, 0)), _const_spec((1, d)),
                _layer_spec(w, l)]
    out_specs, out_shape, scratch = [], [], []
    for (_, width, r, dt) in outs:
        if r == 1:
            out_specs.append(pl.BlockSpec((None, tm, width), lambda b, j: (b, j, 0)))
            out_shape.append(jax.ShapeDtypeStruct((bsz, t, width), dt))
        else:
            assert (tm // r) % 16 == 0
            out_specs.append(pl.BlockSpec((None, r, tm // r, width), lambda b, j: (b, 0, j, 0)))
            out_shape.append(jax.ShapeDtypeStruct((bsz, r, t // r, width), dt))
            scratch.append(pltpu.VMEM((width // LANES, tm, LANES), F32))
    return pl.pallas_call(
        functools.partial(_norm_mm_kernel, outs=tuple(outs)),
        grid=(bsz, t // tm),
        in_specs=in_specs,
        out_specs=out_specs,
        out_shape=out_shape,
        scratch_shapes=scratch,
        compiler_params=_params(2),
        name="norm_mm",
    )(x, g.reshape(1, d), w)


def _proj_ffn_kernel(*refs, combine, final, d_ff):
    it = iter(refs)
    x_ref = next(it)
    if combine:
        o_refs = [next(it) for _ in range(B_N_GROUPS)]
        l_refs = [next(it) for _ in range(B_N_GROUPS)]
    else:
        o_ref = next(it)
    wo_ref, g_ref, wgu_ref, wdn_ref = next(it), next(it), next(it), next(it)
    gfin_ref = next(it) if final else None
    out_ref = next(it)

    if combine:
        ls = [jnp.concatenate([r[j] for j in range(r.shape[0])], axis=1) for r in l_refs]
        mx = jnp.maximum(jnp.maximum(ls[0], ls[1]), ls[2])
        es = [jnp.exp2(l - mx) for l in ls]
        den = es[0] + es[1] + es[2]
        num = es[0] * o_refs[0][...].astype(F32)
        num = num + es[1] * o_refs[1][...].astype(F32)
        num = num + es[2] * o_refs[2][...].astype(F32)
        o = (num / den).astype(BF16)
    else:
        o = o_ref[...]
    x = x_ref[...] + jnp.dot(o, wo_ref[...], preferred_element_type=F32)
    h = _rms(x, g_ref[...]).astype(BF16)
    gu = jnp.dot(h, wgu_ref[...], preferred_element_type=F32)
    gate = gu[:, :d_ff]
    up = gu[:, d_ff:]
    act = (gate / (1.0 + jnp.exp(-gate))) * up
    y = x + jnp.dot(act.astype(BF16), wdn_ref[...], preferred_element_type=F32)
    if final:
        y = _rms(y, gfin_ref[...])
    out_ref[...] = y


def proj_ffn(x, o, lses, w_o, lo, g, w_gu, w_dn, l, g_final=None, tm=256):
    rows, d = x.shape
    combine = lses is not None
    final = g_final is not None
    d_ff = w_dn.shape[1]
    ow = w_o.shape[1]
    row_spec = lambda w: pl.BlockSpec((tm, w), lambda i: (i, 0))
    args, in_specs = [x], [row_spec(d)]
    if combine:
        args += list(o) + list(lses)
        tpb = lses[0].shape[2] // tm
        in_specs += [row_spec(ow)] * B_N_GROUPS
        in_specs += [pl.BlockSpec((None, ow // LANES, tm, LANES),
                                  lambda i: (i // tpb, 0, i % tpb, 0))] * B_N_GROUPS
    else:
        args.append(o)
        in_specs.append(row_spec(ow))
    args += [w_o, g.reshape(1, d), w_gu, w_dn]
    in_specs += [_layer_spec(w_o, lo), _const_spec((1, d)), _layer_spec(w_gu, l), _layer_spec(w_dn, l)]
    if final:
        args.append(g_final.reshape(1, d))
        in_specs.append(_const_spec((1, d)))
    return pl.pallas_call(
        functools.partial(_proj_ffn_kernel, combine=combine, final=final, d_ff=d_ff),
        grid=(rows // tm,),
        in_specs=in_specs,
        out_specs=row_spec(d),
        out_shape=jax.ShapeDtypeStruct((rows, d), F32),
        compiler_params=_params(1),
        name="proj_ffn",
    )(*args)


def _band_attn_kernel(*refs, units, n_kv_pairs, has_sink, emit_lse, r, max_dist, unroll):
    it = iter(refs)
    sink_ref = next(it) if has_sink else None
    q_ref, kc_ref, kp_ref, vc_ref, vp_ref = (next(it) for _ in range(5))
    o_ref = next(it)
    lse_ref = next(it) if emit_lse else None
    bias_ref = next(it)
    o_scr = next(it) if r > 1 else None
    n_blocks = SUPER // BLOCK
    nbeta = n_blocks // r
    step = pl.program_id(1)

    @pl.when((pl.program_id(0) == 0) & (step == 0))
    def _():
        qi = lax.broadcasted_iota(jnp.int32, (2 * BLOCK, 2 * BLOCK), 0) & (BLOCK - 1)
        kj = lax.broadcasted_iota(jnp.int32, (2 * BLOCK, 2 * BLOCK), 1)
        row = lax.broadcasted_iota(jnp.int32, (2 * BLOCK, 2 * BLOCK), 0)
        dist = BLOCK + qi - kj
        valid = (dist >= 0) & (dist <= max_dist)
        distf = (r * dist).astype(F32)
        for u, (_, _, slope_lo, slope_hi, _, _) in enumerate(units):
            slope = jnp.where(row < BLOCK, slope_lo * LOG2E, slope_hi * LOG2E)
            b = -(slope * distf)
            bias_ref[0, u] = jnp.where(valid, b, NEG)
            bias_ref[1, u] = jnp.where(valid & (kj >= BLOCK), b, NEG)

    lane = lax.broadcasted_iota(jnp.int32, (BLOCK, LANES), 1)
    lo = lane < HALF
    row1 = lax.broadcasted_iota(jnp.int32, (2 * BLOCK, 1), 0)

    def block(blk, carry):
        if nbeta == 1:
            c, beta = blk, 0
        elif r == 1:
            c, beta = 0, blk
        else:
            c = lax.shift_right_logical(blk, nbeta.bit_length() - 1)
            beta = blk & (nbeta - 1)
        if nbeta == 1:
            row0 = prow0 = 0
        else:
            row0 = pl.multiple_of(beta * BLOCK, BLOCK)
            prow0 = pl.multiple_of(jnp.maximum(beta - 1, 0) * BLOCK, BLOCK)
        first_beta = beta == 0
        sel = jnp.where(first_beta & (step == 0), 1, 0)

        def prev_cur(cur_ref, prev_ref, sl):
            cur = cur_ref[c, pl.ds(row0, BLOCK), sl]
            if nbeta == 1:
                prev = prev_ref[c, :, sl]
            else:
                prev = jnp.where(first_beta, prev_ref[c, :, sl], cur_ref[c, pl.ds(prow0, BLOCK), sl])
            return jnp.concatenate([prev, cur], axis=0).astype(BF16)

        kcat, vcat = [], []
        for p in range(n_kv_pairs):
            sl = slice(p * LANES, (p + 1) * LANES)
            kcat.append(prev_cur(kc_ref, kp_ref, sl))
            vcat.append(prev_cur(vc_ref, vp_ref, sl))

        for u, (qoff, p, _, _, sink_lo, sink_hi) in enumerate(units):
            cols = slice(qoff, qoff + LANES)
            q = q_ref[c, pl.ds(row0, BLOCK), cols].astype(F32)
            q2 = jnp.concatenate([jnp.where(lo, q, 0.0), jnp.where(lo, 0.0, q)], axis=0).astype(BF16)
            s = lax.dot_general(q2, kcat[p], NT_DIMS, preferred_element_type=F32)
            t = s + bias_ref[sel, u]
            m = jnp.max(t, axis=1, keepdims=True)
            e = jnp.exp2(t - m)
            l = jnp.sum(e, axis=1, keepdims=True)
            if has_sink:
                sink = jnp.where(row1 < BLOCK, sink_ref[sink_lo], sink_ref[sink_hi])
                l = l + jnp.exp2(sink * LOG2E - m)
            pv = jnp.dot(e.astype(BF16), vcat[p], preferred_element_type=F32)
            o2 = pv * (1.0 / l)
            o_sub = jnp.where(lo, o2[:BLOCK], o2[BLOCK:])
            if emit_lse:
                lse2 = jnp.broadcast_to(m + jnp.log2(l), (2 * BLOCK, LANES))
                lse_sub = jnp.where(lo, lse2[:BLOCK], lse2[BLOCK:])
            jq = qoff // LANES
            if r == 1:
                o_ref[pl.ds(row0, BLOCK), cols] = o_sub.astype(o_ref.dtype)
                if emit_lse:
                    lse_ref[jq, pl.ds(row0, BLOCK), :] = lse_sub
            else:
                rows = pl.ds(beta * (BLOCK * r) + c, BLOCK, stride=r)
                o_scr[jq, rows, :] = o_sub
                if emit_lse:
                    lse_ref[jq, rows, :] = lse_sub
        return carry

    lax.fori_loop(0, n_blocks, block, 0, unroll=unroll)
    if r > 1:
        for jq in range(o_scr.shape[0]):
            o_ref[:, jq * LANES:(jq + 1) * LANES] = o_scr[jq].astype(o_ref.dtype)


def band_attn(q, k, v, sinks, units, r, max_dist, emit_lse, unroll=1):
    bsz, _, tr, qw = q.shape
    t = tr * r
    kw = k.shape[3]
    sr = SUPER // r
    has_sink = sinks is not None
    cur = lambda w: pl.BlockSpec((None, r, sr, w), lambda b, i: (b, 0, i, 0))
    prev = lambda w: pl.BlockSpec((None, r, BLOCK, w),
                                  lambda b, i: (b, 0, jnp.maximum(i * (sr // BLOCK) - 1, 0), 0))
    nat = lambda w: pl.BlockSpec((None, SUPER, w), lambda b, i: (b, i, 0))
    args, in_specs = [], []
    if has_sink:
        args.append(sinks)
        in_specs.append(pl.BlockSpec(memory_space=pltpu.SMEM))
    args += [q, k, k, v, v]
    in_specs += [cur(qw), cur(kw), prev(kw), cur(kw), prev(kw)]
    out_shape = [jax.ShapeDtypeStruct((bsz, t, qw), BF16)]
    out_specs = [nat(qw)]
    if emit_lse:
        out_shape.append(jax.ShapeDtypeStruct((bsz, qw // LANES, t, LANES), F32))
        out_specs.append(pl.BlockSpec((None, qw // LANES, SUPER, LANES), lambda b, i: (b, 0, i, 0)))
    scratch = [pltpu.VMEM((2, len(units), 2 * BLOCK, 2 * BLOCK), F32)]
    if r > 1:
        scratch.append(pltpu.VMEM((qw // LANES, SUPER, LANES), F32))
    res = pl.pallas_call(
        functools.partial(_band_attn_kernel, units=tuple(units), n_kv_pairs=kw // LANES,
                          has_sink=has_sink, emit_lse=emit_lse, r=r, max_dist=max_dist,
                          unroll=unroll),
        grid=(bsz, t // SUPER),
        in_specs=in_specs,
        out_specs=out_specs,
        out_shape=out_shape,
        scratch_shapes=scratch,
        compiler_params=_params(2),
        name="band_attn",
    )(*args)
    return res if emit_lse else res[0]


def _units_a():
    units = []
    for g in range(A_GROUP):
        for p in range(A_KV_HEADS // 2):
            h_lo = (2 * p) * A_GROUP + g
            h_hi = (2 * p + 1) * A_GROUP + g
            units.append((g * A_KV_W + p * LANES, p, SLOPES_A[h_lo], SLOPES_A[h_hi], h_lo, h_hi))
    return units


def _units_b(gi):
    units = []
    for p in range(B_SLOTS // 2):
        units.append((p * LANES, p, SLOPES_B[gi * B_SLOTS + 2 * p], SLOPES_B[gi * B_SLOTS + 2 * p + 1],
                      0, 0))
    return units


def _sample_a_kernel(sink_ref, q_ref, kt_ref, vt_ref, knt_ref, vnt_ref, o_ref, ko_ref, vo_ref,
                     *, bb, n_new):
    w = kt_ref.shape[-1]
    rows = A_GROUP * n_new
    assert n_new & (n_new - 1) == 0
    s_idx = lax.broadcasted_iota(jnp.int32, (rows, w), 0) & (n_new - 1)
    row1 = lax.broadcasted_iota(jnp.int32, (rows, 1), 0)
    j = lax.broadcasted_iota(jnp.int32, (rows, w), 1)
    dist_c = w + s_idx - j
    valid_c = (dist_c >= 0) & (dist_c < A_WINDOW)
    dist_n = s_idx - (j - (w - n_new))
    valid_n = (j >= w - n_new) & (dist_n >= 0) & (dist_n < A_WINDOW)
    lane_new = lax.broadcasted_iota(jnp.int32, (HEAD_DIM, w), 1) >= w - n_new

    for kh in range(A_KV_HEADS):
        slope = jnp.zeros((rows, 1), F32)
        sink = jnp.zeros((rows, 1), F32)
        for g in range(A_GROUP):
            slope = jnp.where(row1 >= g * n_new, SLOPES_A[kh * A_GROUP + g] * LOG2E, slope)
            sink = jnp.where(row1 >= g * n_new, sink_ref[kh * A_GROUP + g] * LOG2E, sink)
        bias_c = jnp.where(valid_c, -(slope * dist_c.astype(F32)), NEG)
        bias_n = jnp.where(valid_n, -(slope * dist_n.astype(F32)), NEG)
        for b in range(bb):
            kt = kt_ref[b, kh]
            vt = vt_ref[b, kh]
            knt = knt_ref[b, kh]
            vnt = vnt_ref[b, kh]
            q = q_ref[b, kh]
            tc = jnp.dot(q, kt.astype(BF16), preferred_element_type=F32) + bias_c
            tn = jnp.dot(q, knt.astype(BF16), preferred_element_type=F32) + bias_n
            m = jnp.maximum(jnp.max(tc, axis=1, keepdims=True), jnp.max(tn, axis=1, keepdims=True))
            ec = jnp.exp2(tc - m)
            en = jnp.exp2(tn - m)
            l = (jnp.sum(ec, axis=1, keepdims=True) + jnp.sum(en, axis=1, keepdims=True)
                 + jnp.exp2(sink - m))
            inv = 1.0 / l
            pc = (ec * inv).astype(BF16)
            pn = (en * inv).astype(BF16)
            ot = (lax.dot_general(vt.astype(BF16), pc, NT_DIMS, preferred_element_type=F32)
                  + lax.dot_general(vnt.astype(BF16), pn, NT_DIMS, preferred_element_type=F32))
            o_ref[b, kh] = ot
            ko_ref[b, kh] = jnp.where(lane_new, knt, pltpu.roll(kt, w - n_new, 1))
            vo_ref[b, kh] = jnp.where(lane_new, vnt, pltpu.roll(vt, w - n_new, 1))


def sample_attn_a(sinks, q, kt, vt, l, knt, vnt, bb=2):
    _, n, kh, d, w = kt.shape
    rows = q.shape[2]
    n_new = rows // A_GROUP
    spec = lambda a, b_: pl.BlockSpec((bb, kh, a, b_), lambda i: (i, 0, 0, 0))
    cache = pl.BlockSpec((None, bb, kh, d, w), lambda i: (l, i, 0, 0, 0))
    return pl.pallas_call(
        functools.partial(_sample_a_kernel, bb=bb, n_new=n_new),
        grid=(n // bb,),
        in_specs=[pl.BlockSpec(memory_space=pltpu.SMEM), spec(rows, d), cache, cache,
                  spec(d, w), spec(d, w)],
        out_specs=[spec(d, rows), spec(d, w), spec(d, w)],
        out_shape=[jax.ShapeDtypeStruct((n, kh, d, rows), F32),
                   jax.ShapeDtypeStruct((n, kh, d, w), F32),
                   jax.ShapeDtypeStruct((n, kh, d, w), F32)],
        compiler_params=_params(1),
        name="sample_attn_a",
    )(sinks, q, kt, vt, knt, vnt)


def _sample_b_kernel(*refs, n_new, emit_cache):
    if emit_cache:
        q_ref, kt_ref, vt_ref, knt_ref, vnt_ref, o_ref, ko_ref, vo_ref = refs
    else:
        q_ref, kt_ref, vt_ref, knt_ref, vnt_ref, o_ref = refs
    w = kt_ref.shape[-1]
    s_c = lax.broadcasted_iota(jnp.int32, (n_new, w), 0)
    j_c = lax.broadcasted_iota(jnp.int32, (n_new, w), 1)
    dist_c = w + s_c - j_c
    s_n = lax.broadcasted_iota(jnp.int32, (n_new, LANES), 0)
    j_n = lax.broadcasted_iota(jnp.int32, (n_new, LANES), 1)
    dist_n = s_n - (j_n - (LANES - n_new))
    negd_c, negd_n, starts = [], [], []
    for (win, r) in B_PATTERNS:
        assert r & (r - 1) == 0
        start = max(w - (-(-win // LANES)) * LANES, 0)
        starts.append(start)
        vc = (dist_c >= 0) & (dist_c <= win) & ((dist_c & (r - 1)) == 0)
        vn = (j_n >= LANES - n_new) & (dist_n >= 0) & (dist_n <= win) & ((dist_n & (r - 1)) == 0)
        negd_c.append(jnp.where(vc, -dist_c.astype(F32), NEG)[:, start:])
        negd_n.append(jnp.where(vn, -dist_n.astype(F32), NEG))
    lane_new = lax.broadcasted_iota(jnp.int32, (HEAD_DIM, LANES), 1) >= LANES - n_new

    for slot in range(B_SLOTS):
        kt = kt_ref[0, slot]
        vt = vt_ref[0, slot]
        knt = knt_ref[0, slot]
        vnt = vnt_ref[0, slot]
        q = q_ref[0, slot]
        sc = jnp.dot(q, kt.astype(BF16), preferred_element_type=F32)
        sn = jnp.dot(q, knt.astype(BF16), preferred_element_type=F32)
        es, ens, ls, lses = [], [], [], []
        for gi in range(B_N_GROUPS):
            slope = SLOPES_B[gi * B_SLOTS + slot] * LOG2E
            rs = slice(gi * n_new, (gi + 1) * n_new)
            tc = sc[rs, starts[gi]:] + negd_c[gi] * slope
            tn = sn[rs] + negd_n[gi] * slope
            m = jnp.maximum(jnp.max(tc, axis=1, keepdims=True), jnp.max(tn, axis=1, keepdims=True))
            ec = jnp.exp2(tc - m)
            en = jnp.exp2(tn - m)
            l = jnp.sum(ec, axis=1, keepdims=True) + jnp.sum(en, axis=1, keepdims=True)
            es.append(ec)
            ens.append(en)
            ls.append(l)
            lses.append(m + jnp.log2(l))
        mx = jnp.maximum(jnp.maximum(lses[0], lses[1]), lses[2])
        ws = [jnp.exp2(x - mx) for x in lses]
        den = ws[0] + ws[1] + ws[2]
        cs = [ws[gi] / (den * ls[gi]) for gi in range(B_N_GROUPS)]
        tiles = []
        for tix in range(w // LANES):
            acc = None
            for gi in range(B_N_GROUPS):
                off = tix * LANES - starts[gi]
                if off >= 0:
                    term = cs[gi] * es[gi][:, off:off + LANES]
                    acc = term if acc is None else acc + term
            tiles.append(acc)
        pc = jnp.concatenate(tiles, axis=1).astype(BF16)
        pn = (cs[0] * ens[0] + cs[1] * ens[1] + cs[2] * ens[2]).astype(BF16)
        ot = (lax.dot_general(vt.astype(BF16), pc, NT_DIMS, preferred_element_type=F32)
              + lax.dot_general(vnt.astype(BF16), pn, NT_DIMS, preferred_element_type=F32))
        o_ref[0, slot] = ot
        if emit_cache:
            for src, new, dst in ((kt, knt, ko_ref), (vt, vnt, vo_ref)):
                rolled = pltpu.roll(src, w - n_new, 1)
                dst[0, slot, :, :w - LANES] = rolled[:, :w - LANES]
                dst[0, slot, :, w - LANES:] = jnp.where(lane_new, new, rolled[:, w - LANES:])


def sample_attn_b(q, kt, vt, knt, vnt, n_new, emit_cache):
    n, slots, d, w = kt.shape
    rows = q.shape[2]
    spec = lambda a, b_: pl.BlockSpec((1, slots, a, b_), lambda i: (i, 0, 0, 0))
    out_specs = [spec(d, n_new)]
    out_shape = [jax.ShapeDtypeStruct((n, slots, d, n_new), F32)]
    if emit_cache:
        out_specs += [spec(d, w), spec(d, w)]
        out_shape += [jax.ShapeDtypeStruct((n, slots, d, w), F32)] * 2
    return pl.pallas_call(
        functools.partial(_sample_b_kernel, n_new=n_new, emit_cache=emit_cache),
        grid=(n,),
        in_specs=[spec(rows, d), spec(d, w), spec(d, w), spec(d, LANES), spec(d, LANES)],
        out_specs=out_specs,
        out_shape=out_shape,
        compiler_params=_params(1),
        name="sample_attn_b",
    )(q, kt, vt, knt, vnt)


def _new_rows_t(x, n, s, heads):
    xt = x.reshape(n, s, heads, HEAD_DIM).transpose(0, 2, 3, 1)
    return jnp.pad(xt, ((0, 0), (0, 0), (0, 0), (LANES - s, 0)))


def kernel(x_prompt, x_sample, cache_a_k, cache_a_v, cache_b_k, cache_b_v, g_attn, g_ffn, w_qkv_a,
           sinks_a, w_o_a, g_kv, w_kv_s, w_q_b, w_o_b, w_gate_up, w_down, g_final):
    bp, tp, d = x_prompt.shape
    bs, ts, _ = x_sample.shape
    n_a = w_qkv_a.shape[0]
    depth = g_attn.shape[0]
    wa = cache_a_k.shape[2]
    wb = cache_b_k.shape[1]
    wa_p = min(A_WINDOW, tp)
    wb_p = min(max(w for w, _ in B_PATTERNS), tp)
    assert wa == LANES and wb % LANES == 0 and tp % SUPER == 0

    xp = x_prompt
    xs = x_sample.reshape(1, bs * ts, d)

    wq = w_qkv_a[:, :, :A_Q_W].reshape(n_a, d, A_KV_HEADS, A_GROUP, HEAD_DIM)
    wq = wq.transpose(0, 1, 3, 2, 4).reshape(n_a, d, A_Q_W) * QSCALE
    w_qkv = jnp.concatenate([wq, w_qkv_a[:, :, A_Q_W:]], axis=2).astype(BF16)
    w_oa = w_o_a.reshape(n_a, A_KV_HEADS, A_GROUP, HEAD_DIM, d).transpose(0, 2, 1, 3, 4)
    w_oa = w_oa.reshape(n_a, A_Q_W, d).astype(BF16)
    w_kv = w_kv_s.astype(BF16)[None]
    w_qb = (w_q_b * QSCALE).astype(BF16)
    w_ob = w_o_b.astype(BF16)
    w_gu = w_gate_up.astype(BF16)
    w_dn = w_down.astype(BF16)

    cak_t = cache_a_k.transpose(0, 1, 3, 4, 2)
    cav_t = cache_a_v.transpose(0, 1, 3, 4, 2)
    cbk_t = cache_b_k.transpose(0, 2, 3, 1)
    cbv_t = cache_b_v.transpose(0, 2, 3, 1)

    units_a = _units_a()
    a_k_p, a_v_p, a_k_s, a_v_s = [], [], [], []
    qkv_outs = ((0, A_Q_W, 1, BF16), (A_Q_W, A_KV_W, 1, F32), (A_Q_W + A_KV_W, A_KV_W, 1, F32))
    qb_outs = tuple((gi * B_W, B_W, r, BF16) for gi, (_, r) in enumerate(B_PATTERNS))
    qb_outs_nat = tuple((gi * B_W, B_W, 1, BF16) for gi in range(B_N_GROUPS))
    kv_outs = [(0, B_W, 1, F32), (B_W, B_W, 1, F32)]
    for _, r in B_PATTERNS:
        kv_outs += [(0, B_W, r, BF16), (B_W, B_W, r, BF16)]

    def flat(x):
        return x.reshape(-1, x.shape[-1])

    for l in range(depth):
        last = l == depth - 1
        gfin = g_final if last else None
        if l < n_a:
            qp, kp, vp = norm_mm(xp, g_attn[l], w_qkv, l, qkv_outs)
            qs, ks_, vs_ = norm_mm(xs, g_attn[l], w_qkv, l, qkv_outs)
            op = band_attn(qp[:, None], kp[:, None], vp[:, None], sinks_a[l], units_a, 1,
                           A_WINDOW - 1, False)
            qs_t = qs.reshape(bs, ts, A_GROUP, A_KV_HEADS, HEAD_DIM).transpose(0, 3, 2, 1, 4)
            qs_t = qs_t.reshape(bs, A_KV_HEADS, A_GROUP * ts, HEAD_DIM)
            knt = _new_rows_t(ks_, bs, ts, A_KV_HEADS)
            vnt = _new_rows_t(vs_, bs, ts, A_KV_HEADS)
            os_t, ko, vo = sample_attn_a(sinks_a[l], qs_t, cak_t, cav_t, l, knt, vnt)
            osm = os_t.reshape(bs, A_KV_HEADS, HEAD_DIM, A_GROUP, ts).transpose(0, 4, 3, 1, 2)
            osm = osm.reshape(bs * ts, A_Q_W).astype(BF16)
            a_k_p.append(kp[:, -wa_p:].reshape(bp, wa_p, A_KV_HEADS, HEAD_DIM))
            a_v_p.append(vp[:, -wa_p:].reshape(bp, wa_p, A_KV_HEADS, HEAD_DIM))
            a_k_s.append(ko.transpose(0, 3, 1, 2))
            a_v_s.append(vo.transpose(0, 3, 1, 2))
            xp = proj_ffn(flat(xp), flat(op), None, w_oa, l, g_ffn[l], w_gu, w_dn, l, gfin)
            xs = proj_ffn(flat(xs), osm, None, w_oa, l, g_ffn[l], w_gu, w_dn, l, gfin)
            xp = xp.reshape(bp, tp, d)
            xs = xs.reshape(1, bs * ts, d)
        else:
            b = l - n_a
            if l == n_a:
                kv = norm_mm(xp, g_kv, w_kv, 0, kv_outs)
                kbp, vbp = kv[0], kv[1]
                k_sub, v_sub = kv[2::2], kv[3::2]
                kbs, vbs = norm_mm(xs, g_kv, w_kv, 0, kv_outs[:2])
                kbnt = _new_rows_t(flat(kbs), bs, ts, B_SLOTS)
                vbnt = _new_rows_t(flat(vbs), bs, ts, B_SLOTS)
            q_groups = norm_mm(xp, g_attn[l], w_qb, b, qb_outs)
            outs, lses = [], []
            for gi, (w, r) in enumerate(B_PATTERNS):
                q_sub = q_groups[gi] if r > 1 else q_groups[gi][:, None]
                k_g = k_sub[gi] if r > 1 else k_sub[gi][:, None]
                v_g = v_sub[gi] if r > 1 else v_sub[gi][:, None]
                o_g, lse_g = band_attn(q_sub, k_g, v_g, None, _units_b(gi), r, w // r, True, unroll=2)
                outs.append(flat(o_g))
                lses.append(lse_g)
            qs_groups = norm_mm(xs, g_attn[l], w_qb, b, qb_outs_nat)
            qs_t = jnp.stack([q.reshape(bs, ts, B_SLOTS, HEAD_DIM) for q in qs_groups], axis=1)
            qs_t = qs_t.transpose(0, 3, 1, 2, 4).reshape(bs, B_SLOTS, B_N_GROUPS * ts, HEAD_DIM)
            qs_t = jnp.pad(qs_t, ((0, 0), (0, 0), (0, -(B_N_GROUPS * ts) % 16), (0, 0)))
            if l == n_a:
                os_t, kcb_t, vcb_t = sample_attn_b(qs_t, cbk_t, cbv_t, kbnt, vbnt, ts, True)
            else:
                os_t = sample_attn_b(qs_t, cbk_t, cbv_t, kbnt, vbnt, ts, False)[0]
            osm = os_t.transpose(0, 3, 1, 2).reshape(bs * ts, B_W).astype(BF16)
            xp = proj_ffn(flat(xp), outs, lses, w_ob, b, g_ffn[l], w_gu, w_dn, l, gfin)
            xs = proj_ffn(flat(xs), osm, None, w_ob, b, g_ffn[l], w_gu, w_dn, l, gfin)
            xp = xp.reshape(bp, tp, d)
            xs = xs.reshape(1, bs * ts, d)

    y_sample = xs.reshape(bs, ts, d)
    b_k_p = kbp[:, -wb_p:].reshape(bp, wb_p, B_SLOTS, HEAD_DIM)
    b_v_p = vbp[:, -wb_p:].reshape(bp, wb_p, B_SLOTS, HEAD_DIM)
    return (xp, y_sample,
            jnp.stack(a_k_p, axis=0), jnp.stack(a_v_p, axis=0), b_k_p, b_v_p,
            jnp.stack(a_k_s, axis=0), jnp.stack(a_v_s, axis=0),
            kcb_t.transpose(0, 3, 1, 2), vcb_t.transpose(0, 3, 1, 2))
```

```python
import functools
import math

import jax
import jax.numpy as jnp
from jax import lax
from jax.experimental import pallas as pl
from jax.experimental.pallas import tpu as pltpu

D_MODEL = 1024
HEAD_DIM = 64
A_HEADS = 16
A_KV_HEADS = 4
A_GROUP = A_HEADS // A_KV_HEADS
A_WINDOW = 128
A_Q_W = A_HEADS * HEAD_DIM
A_KV_W = A_KV_HEADS * HEAD_DIM
B_SLOTS = 8
B_PATTERNS = ((128, 1), (512, 4), (2048, 16))
B_N_GROUPS = len(B_PATTERNS)
B_HEADS = B_N_GROUPS * B_SLOTS
B_W = B_SLOTS * HEAD_DIM
BLOCK = 128
EPS = 1e-6
NEG = -1e30
LOG2E = math.log2(math.e)
LN2 = math.log(2.0)
QSCALE = HEAD_DIM ** -0.5 * LOG2E

LANES = 128
HALF = LANES // 2
SUPER = 16 * BLOCK
VMEM_LIMIT = 56 * 1024 * 1024

F32 = jnp.float32
BF16 = jnp.bfloat16
NT_DIMS = (((1,), (1,)), ((), ()))


def _slopes(n):
    return [float(2.0 ** (-8.0 * (i + 1) / n)) for i in range(n)]


SLOPES_A = _slopes(A_HEADS)
SLOPES_B = _slopes(B_HEADS)


def _rms(x, g):
    ms = jnp.mean(x * x, axis=-1, keepdims=True)
    return (x * lax.rsqrt(ms + EPS)) * g


def _params(n_grid):
    return pltpu.CompilerParams(dimension_semantics=("arbitrary",) * n_grid,
                                vmem_limit_bytes=VMEM_LIMIT)


def _const_spec(shape):
    nd = len(shape)
    return pl.BlockSpec(shape, lambda *_: (0,) * nd, pipeline_mode=pl.Buffered(1))


def _layer_spec(w, l):
    return pl.BlockSpec((None,) + w.shape[1:], lambda *_: (l, 0, 0), pipeline_mode=pl.Buffered(1))


def _norm_mm_kernel(x_ref, g_ref, w_ref, *refs, outs):
    out_refs, scr_refs = refs[:len(outs)], refs[len(outs):]
    h = _rms(x_ref[...], g_ref[...])
    y = jnp.dot(h.astype(BF16), w_ref[...], preferred_element_type=F32)
    si = 0
    for o_ref, (off, width, r, _) in zip(out_refs, outs):
        yo = y[:, off:off + width]
        if r == 1:
            o_ref[...] = yo.astype(o_ref.dtype)
        else:
            scr = scr_refs[si]
            si += 1
            rows = yo.shape[0] // r
            for j in range(width // LANES):
                cols = slice(j * LANES, (j + 1) * LANES)
                scr[j] = yo[:, cols]
                for c in range(r):
                    o_ref[c, :, cols] = scr[j, pl.ds(c, rows, stride=r), :].astype(o_ref.dtype)


def norm_mm(x, g, w, l, outs, tm=512):
    bsz, t, d = x.shape
    assert t % tm == 0
    in_specs = [pl.BlockSpec((None, tm, d), lambda b, j: (b, j, 0)), _const_spec((1, d)),
                _layer_spec(w, l)]
    out_specs, out_shape, scratch = [], [], []
    for (_, width, r, dt) in outs:
        if r == 1:
            out_specs.append(pl.BlockSpec((None, tm, width), lambda b, j: (b, j, 0)))
            out_shape.append(jax.ShapeDtypeStruct((bsz, t, width), dt))
        else:
            assert (tm // r) % 16 == 0
            out_specs.append(pl.BlockSpec((None, r, tm // r, width), lambda b, j: (b, 0, j, 0)))
            out_shape.append(jax.ShapeDtypeStruct((bsz, r, t // r, width), dt))
            scratch.append(pltpu.VMEM((width // LANES, tm, LANES), F32))
    return pl.pallas_call(
        functools.partial(_norm_mm_kernel, outs=tuple(outs)),
        grid=(bsz, t // tm),
        in_specs=in_specs,
        out_specs=out_specs,
        out_shape=out_shape,
        scratch_shapes=scratch,
        compiler_params=_params(2),
        name="norm_mm",
    )(x, g.reshape(1, d), w)


def _proj_ffn_kernel(*refs, combine, final, d_ff):
    it = iter(refs)
    x_ref = next(it)
    if combine:
        o_refs = [next(it) for _ in range(B_N_GROUPS)]
        l_refs = [next(it) for _ in range(B_N_GROUPS)]
    else:
        o_ref = next(it)
    wo_ref, g_ref, wgu_ref, wdn_ref = next(it), next(it), next(it), next(it)
    gfin_ref = next(it) if final else None
    out_ref = next(it)

    if combine:
        ls = [jnp.concatenate([r[j] for j in range(r.shape[0])], axis=1) for r in l_refs]
        mx = jnp.maximum(jnp.maximum(ls[0], ls[1]), ls[2])
        es = [jnp.exp2(l - mx) for l in ls]
        den = es[0] + es[1] + es[2]
        num = es[0] * o_refs[0][...].astype(F32)
        num = num + es[1] * o_refs[1][...].astype(F32)
        num = num + es[2] * o_refs[2][...].astype(F32)
        o = (num / den).astype(BF16)
    else:
        o = o_ref[...].astype(BF16)
    x = x_ref[...] + jnp.dot(o, wo_ref[...], preferred_element_type=F32)
    h = _rms(x, g_ref[...]).astype(BF16)
    gu = jnp.dot(h, wgu_ref[...], preferred_element_type=F32)
    gate = gu[:, :d_ff]
    up = gu[:, d_ff:]
    act = (gate / (1.0 + jnp.exp(-gate))) * up
    y = x + jnp.dot(act.astype(BF16), wdn_ref[...], preferred_element_type=F32)
    if final:
        y = _rms(y, gfin_ref[...])
    out_ref[...] = y


def proj_ffn(x, o, lses, w_o, lo, g, w_gu, w_dn, l, g_final=None, tm=256):
    rows, d = x.shape
    combine = lses is not None
    final = g_final is not None
    d_ff = w_dn.shape[1]
    ow = w_o.shape[1]
    row_spec = lambda w: pl.BlockSpec((tm, w), lambda i: (i, 0))
    args, in_specs = [x], [row_spec(d)]
    if combine:
        args += list(o) + list(lses)
        tpb = lses[0].shape[2] // tm
        in_specs += [row_spec(ow)] * B_N_GROUPS
        in_specs += [pl.BlockSpec((None, ow // LANES, tm, LANES),
                                  lambda i: (i // tpb, 0, i % tpb, 0))] * B_N_GROUPS
    else:
        args.append(o)
        in_specs.append(row_spec(ow))
    args += [w_o, g.reshape(1, d), w_gu, w_dn]
    in_specs += [_layer_spec(w_o, lo), _const_spec((1, d)), _layer_spec(w_gu, l), _layer_spec(w_dn, l)]
    if final:
        args.append(g_final.reshape(1, d))
        in_specs.append(_const_spec((1, d)))
    return pl.pallas_call(
        functools.partial(_proj_ffn_kernel, combine=combine, final=final, d_ff=d_ff),
        grid=(rows // tm,),
        in_specs=in_specs,
        out_specs=row_spec(d),
        out_shape=jax.ShapeDtypeStruct((rows, d), F32),
        compiler_params=_params(1),
        name="proj_ffn",
    )(*args)


def _band_attn_kernel(*refs, units, n_kv_pairs, has_sink, emit_lse, r, max_dist, unroll):
    it = iter(refs)
    sink_ref = next(it) if has_sink else None
    q_ref, kc_ref, kp_ref, vc_ref, vp_ref = (next(it) for _ in range(5))
    o_ref = next(it)
    lse_ref = next(it) if emit_lse else None
    bias_ref = next(it)
    o_scr = next(it) if r > 1 else None
    n_blocks = SUPER // BLOCK
    nbeta = n_blocks // r
    step = pl.program_id(1)

    @pl.when((pl.program_id(0) == 0) & (step == 0))
    def _():
        qi = lax.broadcasted_iota(jnp.int32, (2 * BLOCK, 2 * BLOCK), 0) & (BLOCK - 1)
        kj = lax.broadcasted_iota(jnp.int32, (2 * BLOCK, 2 * BLOCK), 1)
        row = lax.broadcasted_iota(jnp.int32, (2 * BLOCK, 2 * BLOCK), 0)
        dist = BLOCK + qi - kj
        valid = (dist >= 0) & (dist <= max_dist)
        distf = (r * dist).astype(F32)
        for u, (_, _, slope_lo, slope_hi, _, _) in enumerate(units):
            slope = jnp.where(row < BLOCK, slope_lo * LOG2E, slope_hi * LOG2E)
            b = -(slope * distf)
            bias_ref[0, u] = jnp.where(valid, b, NEG)
            bias_ref[1, u] = jnp.where(valid & (kj >= BLOCK), b, NEG)

    lane = lax.broadcasted_iota(jnp.int32, (BLOCK, LANES), 1)
    lo = lane < HALF
    row1 = lax.broadcasted_iota(jnp.int32, (2 * BLOCK, 1), 0)

    def block(blk, carry):
        if nbeta == 1:
            c, beta = blk, 0
        elif r == 1:
            c, beta = 0, blk
        else:
            c = lax.shift_right_logical(blk, nbeta.bit_length() - 1)
            beta = blk & (nbeta - 1)
        if nbeta == 1:
            row0 = prow0 = 0
        else:
            row0 = pl.multiple_of(beta * BLOCK, BLOCK)
            prow0 = pl.multiple_of(jnp.maximum(beta - 1, 0) * BLOCK, BLOCK)
        first_beta = beta == 0
        sel = jnp.where(first_beta & (step == 0), 1, 0)

        def prev_cur(cur_ref, prev_ref, sl):
            cur = cur_ref[c, pl.ds(row0, BLOCK), sl]
            if nbeta == 1:
                prev = prev_ref[c, :, sl]
            else:
                prev = jnp.where(first_beta, prev_ref[c, :, sl], cur_ref[c, pl.ds(prow0, BLOCK), sl])
            return jnp.concatenate([prev, cur], axis=0).astype(BF16)

        kcat, vcat = [], []
        for p in range(n_kv_pairs):
            sl = slice(p * LANES, (p + 1) * LANES)
            kcat.append(prev_cur(kc_ref, kp_ref, sl))
            vcat.append(prev_cur(vc_ref, vp_ref, sl))

        for u, (qoff, p, _, _, sink_lo, sink_hi) in enumerate(units):
            cols = slice(qoff, qoff + LANES)
            q = q_ref[c, pl.ds(row0, BLOCK), cols].astype(F32)
            q2 = jnp.concatenate([jnp.where(lo, q, 0.0), jnp.where(lo, 0.0, q)], axis=0).astype(BF16)
            s = lax.dot_general(q2, kcat[p], NT_DIMS, preferred_element_type=F32)
            t = s + bias_ref[sel, u]
            m = jnp.max(t, axis=1, keepdims=True)
            e = jnp.exp2(t - m)
            l = jnp.sum(e, axis=1, keepdims=True)
            if has_sink:
                sink = jnp.where(row1 < BLOCK, sink_ref[sink_lo], sink_ref[sink_hi])
                l = l + jnp.exp2(sink * LOG2E - m)
            pv = jnp.dot(e.astype(BF16), vcat[p], preferred_element_type=F32)
            o2 = pv * (1.0 / l)
            o_sub = jnp.where(lo, o2[:BLOCK], o2[BLOCK:])
            if emit_lse:
                lse2 = jnp.broadcast_to(m + jnp.log2(l), (2 * BLOCK, LANES))
                lse_sub = jnp.where(lo, lse2[:BLOCK], lse2[BLOCK:])
            jq = qoff // LANES
            if r == 1:
                o_ref[pl.ds(row0, BLOCK), cols] = o_sub.astype(o_ref.dtype)
                if emit_lse:
                    lse_ref[jq, pl.ds(row0, BLOCK), :] = lse_sub
            else:
                rows = pl.ds(beta * (BLOCK * r) + c, BLOCK, stride=r)
                o_scr[jq, rows, :] = o_sub
                if emit_lse:
                    lse_ref[jq, rows, :] = lse_sub
        return carry

    lax.fori_loop(0, n_blocks, block, 0, unroll=unroll)
    if r > 1:
        for jq in range(o_scr.shape[0]):
            o_ref[:, jq * LANES:(jq + 1) * LANES] = o_scr[jq].astype(o_ref.dtype)


def band_attn(q, k, v, sinks, units, r, max_dist, emit_lse, unroll=1):
    bsz, _, tr, qw = q.shape
    t = tr * r
    kw = k.shape[3]
    sr = SUPER // r
    has_sink = sinks is not None
    cur = lambda w: pl.BlockSpec((None, r, sr, w), lambda b, i: (b, 0, i, 0))
    prev = lambda w: pl.BlockSpec((None, r, BLOCK, w),
                                  lambda b, i: (b, 0, jnp.maximum(i * (sr // BLOCK) - 1, 0), 0))
    nat = lambda w: pl.BlockSpec((None, SUPER, w), lambda b, i: (b, i, 0))
    args, in_specs = [], []
    if has_sink:
        args.append(sinks)
        in_specs.append(pl.BlockSpec(memory_space=pltpu.SMEM))
    args += [q, k, k, v, v]
    in_specs += [cur(qw), cur(kw), prev(kw), cur(kw), prev(kw)]
    out_shape = [jax.ShapeDtypeStruct((bsz, t, qw), BF16)]
    out_specs = [nat(qw)]
    if emit_lse:
        out_shape.append(jax.ShapeDtypeStruct((bsz, qw // LANES, t, LANES), F32))
        out_specs.append(pl.BlockSpec((None, qw // LANES, SUPER, LANES), lambda b, i: (b, 0, i, 0)))
    scratch = [pltpu.VMEM((2, len(units), 2 * BLOCK, 2 * BLOCK), F32)]
    if r > 1:
        scratch.append(pltpu.VMEM((qw // LANES, SUPER, LANES), F32))
    res = pl.pallas_call(
        functools.partial(_band_attn_kernel, units=tuple(units), n_kv_pairs=kw // LANES,
                          has_sink=has_sink, emit_lse=emit_lse, r=r, max_dist=max_dist,
                          unroll=unroll),
        grid=(bsz, t // SUPER),
        in_specs=in_specs,
        out_specs=out_specs,
        out_shape=out_shape,
        scratch_shapes=scratch,
        compiler_params=_params(2),
        name="band_attn",
    )(*args)
    return res if emit_lse else res[0]


def _units_a():
    units = []
    for g in range(A_GROUP):
        for p in range(A_KV_HEADS // 2):
            h_lo = (2 * p) * A_GROUP + g
            h_hi = (2 * p + 1) * A_GROUP + g
            units.append((g * A_KV_W + p * LANES, p, SLOPES_A[h_lo], SLOPES_A[h_hi], h_lo, h_hi))
    return units


def _units_b(gi):
    units = []
    for p in range(B_SLOTS // 2):
        units.append((p * LANES, p, SLOPES_B[gi * B_SLOTS + 2 * p], SLOPES_B[gi * B_SLOTS + 2 * p + 1],
                      0, 0))
    return units


def _lane_group_mask(rows, width, group):
    lane = lax.broadcasted_iota(jnp.int32, (rows, width), 1)
    return (lane >= group * HEAD_DIM) & (lane < (group + 1) * HEAD_DIM)


def _block_diag_rows(q, n_heads):
    s, width = q.shape
    return jnp.concatenate(
        [jnp.where(_lane_group_mask(s, width, h), q, 0.0) for h in range(n_heads)], axis=0)


def _diag_rows(o_bd, n_heads):
    s = o_bd.shape[0] // n_heads
    width = o_bd.shape[1]
    acc = None
    for h in range(n_heads):
        term = jnp.where(_lane_group_mask(s, width, h), o_bd[h * s:(h + 1) * s], 0.0)
        acc = term if acc is None else acc + term
    return acc


def _pad_new(new_rows):
    s, c = new_rows.shape
    return jnp.concatenate([jnp.zeros((LANES - s, c), F32), new_rows], axis=0)


def _shifted(cache_t, new_pad, s):
    c, w = cache_t.shape
    rolled = pltpu.roll(cache_t, w - s, 1)
    lane_new = lax.broadcasted_iota(jnp.int32, (c, LANES), 1) >= LANES - s
    last = jnp.where(lane_new, new_pad.T, rolled[:, w - LANES:])
    if w == LANES:
        return last
    return jnp.concatenate([rolled[:, :w - LANES], last], axis=1)


def _sample_a_kernel(sink_ref, q_ref, kn_ref, vn_ref, kt_ref, vt_ref, o_ref, ko_ref, vo_ref,
                     bias_c_ref, bias_n_ref, *, bb, s):
    n_rows = A_HEADS * s
    w = kt_ref.shape[-1]
    row1 = lax.broadcasted_iota(jnp.int32, (n_rows, 1), 0)

    @pl.when(pl.program_id(0) == 0)
    def _():
        row = lax.broadcasted_iota(jnp.int32, (n_rows, w), 0)
        tok = row & (s - 1)
        j = lax.broadcasted_iota(jnp.int32, (n_rows, w), 1)
        slope = jnp.zeros((n_rows, w), F32)
        for g in range(A_GROUP):
            for kh in range(A_KV_HEADS):
                slope = jnp.where(row >= (g * A_KV_HEADS + kh) * s,
                                  SLOPES_A[kh * A_GROUP + g] * LOG2E, slope)
        dist_c = w + tok - j
        bias_c_ref[...] = jnp.where((dist_c >= 0) & (dist_c < A_WINDOW),
                                    -(slope * dist_c.astype(F32)), NEG)
        dist_n = tok - (j - (LANES - s))
        bias_n_ref[...] = jnp.where((j >= LANES - s) & (dist_n >= 0) & (dist_n < A_WINDOW),
                                    -(slope * dist_n.astype(F32)), NEG)

    sink = jnp.zeros((n_rows, 1), F32)
    for g in range(A_GROUP):
        for kh in range(A_KV_HEADS):
            sink = jnp.where(row1 >= (g * A_KV_HEADS + kh) * s,
                             sink_ref[kh * A_GROUP + g] * LOG2E, sink)
    bias_c = bias_c_ref[...]
    bias_n = bias_n_ref[...]

    outs = []
    for b in range(bb):
        rows = slice(b * s, (b + 1) * s)
        q = q_ref[rows, :]
        q_bd = jnp.concatenate(
            [_block_diag_rows(q[:, g * A_KV_W:(g + 1) * A_KV_W], A_KV_HEADS) for g in range(A_GROUP)],
            axis=0).astype(BF16)
        kt = kt_ref[b].reshape(A_KV_W, w)
        vt = vt_ref[b].reshape(A_KV_W, w)
        kn = _pad_new(kn_ref[rows, :])
        vn = _pad_new(vn_ref[rows, :])
        tc = jnp.dot(q_bd, kt.astype(BF16), preferred_element_type=F32) + bias_c
        tn = lax.dot_general(q_bd, kn.astype(BF16), NT_DIMS, preferred_element_type=F32) + bias_n
        m = jnp.maximum(jnp.max(tc, axis=1, keepdims=True), jnp.max(tn, axis=1, keepdims=True))
        ec = jnp.exp2(tc - m)
        en = jnp.exp2(tn - m)
        l = (jnp.sum(ec, axis=1, keepdims=True) + jnp.sum(en, axis=1, keepdims=True)
             + jnp.exp2(sink - m))
        inv = 1.0 / l
        o_bd = (lax.dot_general((ec * inv).astype(BF16), vt.astype(BF16), NT_DIMS,
                                preferred_element_type=F32)
                + jnp.dot((en * inv).astype(BF16), vn.astype(BF16), preferred_element_type=F32))
        grp = A_KV_HEADS * s
        outs.append(jnp.concatenate(
            [_diag_rows(o_bd[g * grp:(g + 1) * grp], A_KV_HEADS) for g in range(A_GROUP)], axis=1))
        ko_ref[b] = _shifted(kt, kn, s).reshape(ko_ref.shape[1:])
        vo_ref[b] = _shifted(vt, vn, s).reshape(vo_ref.shape[1:])
    o_ref[...] = jnp.concatenate(outs, axis=0).astype(o_ref.dtype)


def sample_attn_a(sinks, q, kn, vn, kt, vt, l, s, bb=4):
    _, n, kh, d, w = kt.shape
    rows_spec = lambda width: pl.BlockSpec((bb * s, width), lambda i: (i, 0))
    cache = pl.BlockSpec((None, bb, kh, d, w), lambda i: (l, i, 0, 0, 0))
    cache_out = pl.BlockSpec((bb, kh, d, w), lambda i: (i, 0, 0, 0))
    return pl.pallas_call(
        functools.partial(_sample_a_kernel, bb=bb, s=s),
        grid=(n // bb,),
        in_specs=[pl.BlockSpec(memory_space=pltpu.SMEM), rows_spec(A_Q_W), rows_spec(A_KV_W),
                  rows_spec(A_KV_W), cache, cache],
        out_specs=[rows_spec(A_Q_W), cache_out, cache_out],
        out_shape=[jax.ShapeDtypeStruct((n * s, A_Q_W), BF16),
                   jax.ShapeDtypeStruct((n, kh, d, w), F32),
                   jax.ShapeDtypeStruct((n, kh, d, w), F32)],
        scratch_shapes=[pltpu.VMEM((A_HEADS * s, w), F32), pltpu.VMEM((A_HEADS * s, LANES), F32)],
        compiler_params=_params(1),
        name="sample_attn_a",
    )(sinks, q, kn, vn, kt, vt)


def _sample_b_kernel(*refs, s, emit_cache):
    it = iter(refs)
    q_refs = [next(it) for _ in range(B_N_GROUPS)]
    kn_ref, vn_ref, kt_ref, vt_ref, o_ref = (next(it) for _ in range(5))
    ko_ref, vo_ref = (next(it), next(it)) if emit_cache else (None, None)
    bias_refs = [next(it) for _ in range(B_N_GROUPS)]
    bias_n_ref = next(it)
    w = kt_ref.shape[-1]
    n_rows = B_SLOTS * s
    starts = [max(w - (-(-win // LANES)) * LANES, 0) for win, _ in B_PATTERNS]

    def bias_table(gi, n, dist_of_lane):
        win, r = B_PATTERNS[gi]
        assert r & (r - 1) == 0
        row = lax.broadcasted_iota(jnp.int32, (n_rows, n), 0)
        lane = lax.broadcasted_iota(jnp.int32, (n_rows, n), 1)
        slope = jnp.zeros((n_rows, n), F32)
        for slot in range(B_SLOTS):
            slope = jnp.where(row >= slot * s, SLOPES_B[gi * B_SLOTS + slot] * LOG2E, slope)
        dist, ok = dist_of_lane(row & (s - 1), lane)
        valid = ok & (dist >= 0) & (dist <= win) & ((dist & (r - 1)) == 0)
        return jnp.where(valid, -(slope * dist.astype(F32)), NEG)

    @pl.when(pl.program_id(0) == 0)
    def _():
        for gi in range(B_N_GROUPS):
            bias_refs[gi][...] = bias_table(
                gi, w - starts[gi], lambda tok, lane: (w + tok - (lane + starts[gi]), lane >= 0))
            bias_n_ref[gi] = bias_table(
                gi, LANES, lambda tok, lane: (tok - (lane - (LANES - s)), lane >= LANES - s))

    kt = kt_ref[0].reshape(B_W, w)
    vt = vt_ref[0].reshape(B_W, w)
    kn = _pad_new(kn_ref[...])
    vn = _pad_new(vn_ref[...])
    kt16 = kt.astype(BF16)
    kn16 = kn.astype(BF16)
    es, ens, ls, lses = [], [], [], []
    for gi in range(B_N_GROUPS):
        q_bd = _block_diag_rows(q_refs[gi][...], B_SLOTS).astype(BF16)
        tc = jnp.dot(q_bd, kt16[:, starts[gi]:], preferred_element_type=F32) + bias_refs[gi][...]
        tn = lax.dot_general(q_bd, kn16, NT_DIMS, preferred_element_type=F32) + bias_n_ref[gi]
        m = jnp.maximum(jnp.max(tc, axis=1, keepdims=True), jnp.max(tn, axis=1, keepdims=True))
        ec = jnp.exp2(tc - m)
        en = jnp.exp2(tn - m)
        l = jnp.sum(ec, axis=1, keepdims=True) + jnp.sum(en, axis=1, keepdims=True)
        es.append(ec)
        ens.append(en)
        ls.append(l)
        lses.append(m + jnp.log2(l))
    mx = jnp.maximum(jnp.maximum(lses[0], lses[1]), lses[2])
    ws = [jnp.exp2(x - mx) for x in lses]
    den = ws[0] + ws[1] + ws[2]
    cs = [ws[gi] / (den * ls[gi]) for gi in range(B_N_GROUPS)]
    tiles = []
    for tix in range(w // LANES):
        acc = None
        for gi in range(B_N_GROUPS):
            off = tix * LANES - starts[gi]
            if off >= 0:
                term = cs[gi] * es[gi][:, off:off + LANES]
                acc = term if acc is None else acc + term
        tiles.append(acc)
    pc = jnp.concatenate(tiles, axis=1).astype(BF16)
    pn = (cs[0] * ens[0] + cs[1] * ens[1] + cs[2] * ens[2]).astype(BF16)
    o_bd = (lax.dot_general(pc, vt.astype(BF16), NT_DIMS, preferred_element_type=F32)
            + jnp.dot(pn, vn.astype(BF16), preferred_element_type=F32))
    o_ref[...] = _diag_rows(o_bd, B_SLOTS)
    if emit_cache:
        ko_ref[0] = _shifted(kt, kn, s).reshape(ko_ref.shape[1:])
        vo_ref[0] = _shifted(vt, vn, s).reshape(vo_ref.shape[1:])


def sample_attn_b(qs, kn, vn, kt, vt, s, emit_cache):
    n, slots, d, w = kt.shape
    rows_spec = pl.BlockSpec((s, B_W), lambda i: (i, 0))
    cache = pl.BlockSpec((1, slots, d, w), lambda i: (i, 0, 0, 0))
    out_specs = [rows_spec]
    out_shape = [jax.ShapeDtypeStruct((n * s, B_W), F32)]
    if emit_cache:
        out_specs += [cache, cache]
        out_shape += [jax.ShapeDtypeStruct((n, slots, d, w), F32)] * 2
    starts = [max(w - (-(-win // LANES)) * LANES, 0) for win, _ in B_PATTERNS]
    scratch = [pltpu.VMEM((slots * s, w - st), F32) for st in starts]
    scratch.append(pltpu.VMEM((B_N_GROUPS, slots * s, LANES), F32))
    return pl.pallas_call(
        functools.partial(_sample_b_kernel, s=s, emit_cache=emit_cache),
        grid=(n,),
        in_specs=[rows_spec] * (B_N_GROUPS + 2) + [cache, cache],
        out_specs=out_specs,
        out_shape=out_shape,
        scratch_shapes=scratch,
        compiler_params=_params(1),
        name="sample_attn_b",
    )(*qs, kn, vn, kt, vt)


def kernel(x_prompt, x_sample, cache_a_k, cache_a_v, cache_b_k, cache_b_v, g_attn, g_ffn, w_qkv_a,
           sinks_a, w_o_a, g_kv, w_kv_s, w_q_b, w_o_b, w_gate_up, w_down, g_final):
    bp, tp, d = x_prompt.shape
    bs, ts, _ = x_sample.shape
    n_a = w_qkv_a.shape[0]
    depth = g_attn.shape[0]
    wa = cache_a_k.shape[2]
    wb = cache_b_k.shape[1]
    wa_p = min(A_WINDOW, tp)
    wb_p = min(max(w for w, _ in B_PATTERNS), tp)
    assert wa == LANES and wb % LANES == 0 and tp % SUPER == 0

    xp = x_prompt
    xs = x_sample.reshape(1, bs * ts, d)

    wq = w_qkv_a[:, :, :A_Q_W].reshape(n_a, d, A_KV_HEADS, A_GROUP, HEAD_DIM)
    wq = wq.transpose(0, 1, 3, 2, 4).reshape(n_a, d, A_Q_W) * QSCALE
    w_qkv = jnp.concatenate([wq, w_qkv_a[:, :, A_Q_W:]], axis=2).astype(BF16)
    w_oa = w_o_a.reshape(n_a, A_KV_HEADS, A_GROUP, HEAD_DIM, d).transpose(0, 2, 1, 3, 4)
    w_oa = w_oa.reshape(n_a, A_Q_W, d).astype(BF16)
    w_kv = w_kv_s.astype(BF16)[None]
    w_qb = (w_q_b * QSCALE).astype(BF16)
    w_ob = w_o_b.astype(BF16)
    w_gu = w_gate_up.astype(BF16)
    w_dn = w_down.astype(BF16)

    cak_t = cache_a_k.transpose(0, 1, 3, 4, 2)
    cav_t = cache_a_v.transpose(0, 1, 3, 4, 2)
    cbk_t = cache_b_k.transpose(0, 2, 3, 1)
    cbv_t = cache_b_v.transpose(0, 2, 3, 1)

    units_a = _units_a()
    a_k_p, a_v_p, a_k_s, a_v_s = [], [], [], []
    qkv_outs = ((0, A_Q_W, 1, BF16), (A_Q_W, A_KV_W, 1, F32), (A_Q_W + A_KV_W, A_KV_W, 1, F32))
    qkv_outs_s = ((0, A_Q_W, 1, F32),) + qkv_outs[1:]
    qb_outs = tuple((gi * B_W, B_W, r, BF16) for gi, (_, r) in enumerate(B_PATTERNS))
    qb_outs_s = tuple((gi * B_W, B_W, 1, F32) for gi in range(B_N_GROUPS))
    kv_outs = [(0, B_W, 1, F32), (B_W, B_W, 1, F32)]
    for _, r in B_PATTERNS:
        kv_outs += [(0, B_W, r, BF16), (B_W, B_W, r, BF16)]

    def flat(x):
        return x.reshape(-1, x.shape[-1])

    for l in range(depth):
        last = l == depth - 1
        gfin = g_final if last else None
        if l < n_a:
            qp, kp, vp = norm_mm(xp, g_attn[l], w_qkv, l, qkv_outs)
            qs, ks_, vs_ = norm_mm(xs, g_attn[l], w_qkv, l, qkv_outs_s)
            op = band_attn(qp[:, None], kp[:, None], vp[:, None], sinks_a[l], units_a, 1,
                           A_WINDOW - 1, False)
            osm, ko, vo = sample_attn_a(sinks_a[l], flat(qs), flat(ks_), flat(vs_), cak_t, cav_t, l, ts)
            a_k_p.append(kp[:, -wa_p:].reshape(bp, wa_p, A_KV_HEADS, HEAD_DIM))
            a_v_p.append(vp[:, -wa_p:].reshape(bp, wa_p, A_KV_HEADS, HEAD_DIM))
            a_k_s.append(ko.transpose(0, 3, 1, 2))
            a_v_s.append(vo.transpose(0, 3, 1, 2))
            xp = proj_ffn(flat(xp), flat(op), None, w_oa, l, g_ffn[l], w_gu, w_dn, l, gfin)
            xs = proj_ffn(flat(xs), osm, None, w_oa, l, g_ffn[l], w_gu, w_dn, l, gfin)
            xp = xp.reshape(bp, tp, d)
            xs = xs.reshape(1, bs * ts, d)
        else:
            b = l - n_a
            if l == n_a:
                kv = norm_mm(xp, g_kv, w_kv, 0, kv_outs)
                kbp, vbp = kv[0], kv[1]
                k_sub, v_sub = kv[2::2], kv[3::2]
                kbs, vbs = norm_mm(xs, g_kv, w_kv, 0, kv_outs[:2])
                kbs, vbs = flat(kbs), flat(vbs)
            q_groups = norm_mm(xp, g_attn[l], w_qb, b, qb_outs)
            outs, lses = [], []
            for gi, (w, r) in enumerate(B_PATTERNS):
                q_sub = q_groups[gi] if r > 1 else q_groups[gi][:, None]
                k_g = k_sub[gi] if r > 1 else k_sub[gi][:, None]
                v_g = v_sub[gi] if r > 1 else v_sub[gi][:, None]
                o_g, lse_g = band_attn(q_sub, k_g, v_g, None, _units_b(gi), r, w // r, True, unroll=2)
                outs.append(flat(o_g))
                lses.append(lse_g)
            qs_groups = [flat(q) for q in norm_mm(xs, g_attn[l], w_qb, b, qb_outs_s)]
            if l == n_a:
                osm, kcb_t, vcb_t = sample_attn_b(qs_groups, kbs, vbs, cbk_t, cbv_t, ts, True)
            else:
                osm = sample_attn_b(qs_groups, kbs, vbs, cbk_t, cbv_t, ts, False)[0]
            xp = proj_ffn(flat(xp), outs, lses, w_ob, b, g_ffn[l], w_gu, w_dn, l, gfin)
            xs = proj_ffn(flat(xs), osm, None, w_ob, b, g_ffn[l], w_gu, w_dn, l, gfin)
            xp = xp.reshape(bp, tp, d)
            xs = xs.reshape(1, bs * ts, d)

    y_sample = xs.reshape(bs, ts, d)
    b_k_p = kbp[:, -wb_p:].reshape(bp, wb_p, B_SLOTS, HEAD_DIM)
    b_v_p = vbp[:, -wb_p:].reshape(bp, wb_p, B_SLOTS, HEAD_DIM)
    return (xp, y_sample,
            jnp.stack(a_k_p, axis=0), jnp.stack(a_v_p, axis=0), b_k_p, b_v_p,
            jnp.stack(a_k_s, axis=0), jnp.stack(a_v_s, axis=0),
            kcb_t.transpose(0, 3, 1, 2), vcb_t.transpose(0, 3, 1, 2))
```

```python
import functools
import math

import jax
import jax.numpy as jnp
from jax import lax
from jax.experimental import pallas as pl
from jax.experimental.pallas import tpu as pltpu

D_MODEL = 1024
HEAD_DIM = 64
A_HEADS = 16
A_KV_HEADS = 4
A_GROUP = A_HEADS // A_KV_HEADS
A_WINDOW = 128
A_Q_W = A_HEADS * HEAD_DIM
A_KV_W = A_KV_HEADS * HEAD_DIM
B_SLOTS = 8
B_PATTERNS = ((128, 1), (512, 4), (2048, 16))
B_N_GROUPS = len(B_PATTERNS)
B_HEADS = B_N_GROUPS * B_SLOTS
B_W = B_SLOTS * HEAD_DIM
BLOCK = 128
EPS = 1e-6
NEG = -1e30
LOG2E = math.log2(math.e)
LN2 = math.log(2.0)
QSCALE = HEAD_DIM ** -0.5 * LOG2E

LANES = 128
HALF = LANES // 2
SUPER = 16 * BLOCK
VMEM_LIMIT = 56 * 1024 * 1024

F32 = jnp.float32
BF16 = jnp.bfloat16
NT_DIMS = (((1,), (1,)), ((), ()))


def _slopes(n):
    return [float(2.0 ** (-8.0 * (i + 1) / n)) for i in range(n)]


SLOPES_A = _slopes(A_HEADS)
SLOPES_B = _slopes(B_HEADS)


def _rms(x, g):
    ms = jnp.mean(x * x, axis=-1, keepdims=True)
    return (x * lax.rsqrt(ms + EPS)) * g


def _params(n_grid):
    return pltpu.CompilerParams(dimension_semantics=("arbitrary",) * n_grid,
                                vmem_limit_bytes=VMEM_LIMIT)


def _const_spec(shape):
    nd = len(shape)
    return pl.BlockSpec(shape, lambda *_: (0,) * nd, pipeline_mode=pl.Buffered(1))


def _layer_spec(w, l):
    return pl.BlockSpec((None,) + w.shape[1:], lambda *_: (l, 0, 0), pipeline_mode=pl.Buffered(1))


def _norm_mm_kernel(x_ref, g_ref, w_ref, *refs, outs):
    out_refs, scr_refs = refs[:len(outs)], refs[len(outs):]
    h = _rms(x_ref[...], g_ref[...])
    y = jnp.dot(h.astype(BF16), w_ref[...], preferred_element_type=F32)
    si = 0
    for o_ref, (off, width, r, _) in zip(out_refs, outs):
        yo = y[:, off:off + width]
        if r <= 1:
            o_ref[...] = yo.astype(o_ref.dtype)
        else:
            scr = scr_refs[si]
            si += 1
            rows = yo.shape[0] // r
            for j in range(width // LANES):
                cols = slice(j * LANES, (j + 1) * LANES)
                scr[j] = yo[:, cols]
                for c in range(r):
                    o_ref[c, :, cols] = scr[j, pl.ds(c, rows, stride=r), :].astype(o_ref.dtype)


def norm_mm(x, g, w, l, outs, tm=512):
    bsz, t, d = x.shape
    assert t % tm == 0
    in_specs = [pl.BlockSpec((None, tm, d), lambda b, j: (b, j, 0)), _const_spec((1, d)),
                _layer_spec(w, l)]
    out_specs, out_shape, scratch = [], [], []
    for (_, width, r, dt) in outs:
        if r == 1:
            out_specs.append(pl.BlockSpec((None, tm, width), lambda b, j: (b, j, 0)))
            out_shape.append(jax.ShapeDtypeStruct((bsz, t, width), dt))
        elif r < 0:
            tail = -r
            assert tail % tm == 0 and tail <= t
            first = (t - tail) // tm
            out_specs.append(pl.BlockSpec((None, tm, width),
                                          lambda b, j, first=first: (b, jnp.maximum(j - first, 0), 0)))
            out_shape.append(jax.ShapeDtypeStruct((bsz, tail, width), dt))
        else:
            assert (tm // r) % 16 == 0
            out_specs.append(pl.BlockSpec((None, r, tm // r, width), lambda b, j: (b, 0, j, 0)))
            out_shape.append(jax.ShapeDtypeStruct((bsz, r, t // r, width), dt))
            scratch.append(pltpu.VMEM((width // LANES, tm, LANES), F32))
    return pl.pallas_call(
        functools.partial(_norm_mm_kernel, outs=tuple(outs)),
        grid=(bsz, t // tm),
        in_specs=in_specs,
        out_specs=out_specs,
        out_shape=out_shape,
        scratch_shapes=scratch,
        compiler_params=_params(2),
        name="norm_mm",
    )(x, g.reshape(1, d), w)


def _proj_ffn_kernel(*refs, combine, final, d_ff):
    it = iter(refs)
    x_ref = next(it)
    if combine:
        o_refs = [next(it) for _ in range(B_N_GROUPS)]
        l_refs = [next(it) for _ in range(B_N_GROUPS)]
    else:
        o_ref = next(it)
    wo_ref, g_ref, wgu_ref, wdn_ref = next(it), next(it), next(it), next(it)
    gfin_ref = next(it) if final else None
    out_ref = next(it)

    if combine:
        ls = [jnp.concatenate([r[j] for j in range(r.shape[0])], axis=1) for r in l_refs]
        mx = jnp.maximum(jnp.maximum(ls[0], ls[1]), ls[2])
        es = [jnp.exp2(l - mx) for l in ls]
        den = es[0] + es[1] + es[2]
        num = es[0] * o_refs[0][...].astype(F32)
        num = num + es[1] * o_refs[1][...].astype(F32)
        num = num + es[2] * o_refs[2][...].astype(F32)
        o = (num / den).astype(BF16)
    else:
        o = o_ref[...].astype(BF16)
    x = x_ref[...] + jnp.dot(o, wo_ref[...], preferred_element_type=F32)
    h = _rms(x, g_ref[...]).astype(BF16)
    gu = jnp.dot(h, wgu_ref[...], preferred_element_type=F32)
    gate = gu[:, :d_ff]
    up = gu[:, d_ff:]
    act = (gate / (1.0 + jnp.exp(-gate))) * up
    y = x + jnp.dot(act.astype(BF16), wdn_ref[...], preferred_element_type=F32)
    if final:
        y = _rms(y, gfin_ref[...])
    out_ref[...] = y


def proj_ffn(x, o, lses, w_o, lo, g, w_gu, w_dn, l, g_final=None, tm=256):
    rows, d = x.shape
    combine = lses is not None
    final = g_final is not None
    d_ff = w_dn.shape[1]
    ow = w_o.shape[1]
    row_spec = lambda w: pl.BlockSpec((tm, w), lambda i: (i, 0))
    args, in_specs = [x], [row_spec(d)]
    if combine:
        args += list(o) + list(lses)
        tpb = lses[0].shape[2] // tm
        in_specs += [row_spec(ow)] * B_N_GROUPS
        in_specs += [pl.BlockSpec((None, ow // LANES, tm, LANES),
                                  lambda i: (i // tpb, 0, i % tpb, 0))] * B_N_GROUPS
    else:
        args.append(o)
        in_specs.append(row_spec(ow))
    args += [w_o, g.reshape(1, d), w_gu, w_dn]
    in_specs += [_layer_spec(w_o, lo), _const_spec((1, d)), _layer_spec(w_gu, l), _layer_spec(w_dn, l)]
    if final:
        args.append(g_final.reshape(1, d))
        in_specs.append(_const_spec((1, d)))
    return pl.pallas_call(
        functools.partial(_proj_ffn_kernel, combine=combine, final=final, d_ff=d_ff),
        grid=(rows // tm,),
        in_specs=in_specs,
        out_specs=row_spec(d),
        out_shape=jax.ShapeDtypeStruct((rows, d), F32),
        compiler_params=_params(1),
        name="proj_ffn",
    )(*args)


def _band_attn_kernel(*refs, units, n_kv_pairs, has_sink, emit_lse, r, max_dist, unroll):
    it = iter(refs)
    sink_ref = next(it) if has_sink else None
    q_ref, kc_ref, kp_ref, vc_ref, vp_ref = (next(it) for _ in range(5))
    o_ref = next(it)
    lse_ref = next(it) if emit_lse else None
    bias_ref = next(it)
    o_scr = next(it) if r > 1 else None
    n_blocks = SUPER // BLOCK
    nbeta = n_blocks // r
    step = pl.program_id(1)

    @pl.when((pl.program_id(0) == 0) & (step == 0))
    def _():
        qi = lax.broadcasted_iota(jnp.int32, (2 * BLOCK, 2 * BLOCK), 0) & (BLOCK - 1)
        kj = lax.broadcasted_iota(jnp.int32, (2 * BLOCK, 2 * BLOCK), 1)
        row = lax.broadcasted_iota(jnp.int32, (2 * BLOCK, 2 * BLOCK), 0)
        dist = BLOCK + qi - kj
        valid = (dist >= 0) & (dist <= max_dist)
        distf = (r * dist).astype(F32)
        for u, (_, _, slope_lo, slope_hi, _, _) in enumerate(units):
            slope = jnp.where(row < BLOCK, slope_lo * LOG2E, slope_hi * LOG2E)
            b = -(slope * distf)
            bias_ref[0, u] = jnp.where(valid, b, NEG)
            bias_ref[1, u] = jnp.where(valid & (kj >= BLOCK), b, NEG)

    lane = lax.broadcasted_iota(jnp.int32, (BLOCK, LANES), 1)
    lo = lane < HALF
    row1 = lax.broadcasted_iota(jnp.int32, (2 * BLOCK, 1), 0)

    def block(blk, carry):
        if nbeta == 1:
            c, beta = blk, 0
        elif r == 1:
            c, beta = 0, blk
        else:
            c = lax.shift_right_logical(blk, nbeta.bit_length() - 1)
            beta = blk & (nbeta - 1)
        if nbeta == 1:
            row0 = prow0 = 0
        else:
            row0 = pl.multiple_of(beta * BLOCK, BLOCK)
            prow0 = pl.multiple_of(jnp.maximum(beta - 1, 0) * BLOCK, BLOCK)
        first_beta = beta == 0
        sel = jnp.where(first_beta & (step == 0), 1, 0)

        def prev_cur(cur_ref, prev_ref, sl):
            cur = cur_ref[c, pl.ds(row0, BLOCK), sl]
            if nbeta == 1:
                prev = prev_ref[c, :, sl]
            else:
                prev = jnp.where(first_beta, prev_ref[c, :, sl], cur_ref[c, pl.ds(prow0, BLOCK), sl])
            return jnp.concatenate([prev, cur], axis=0).astype(BF16)

        kcat, vcat = [], []
        for p in range(n_kv_pairs):
            sl = slice(p * LANES, (p + 1) * LANES)
            kcat.append(prev_cur(kc_ref, kp_ref, sl))
            vcat.append(prev_cur(vc_ref, vp_ref, sl))

        heads = [(u, h) for u in range(len(units)) for h in range(2)]

        def scores(u, h):
            qoff, p = units[u][0], units[u][1]
            q = q_ref[c, pl.ds(row0, BLOCK), qoff:qoff + LANES].astype(F32)
            qh = (jnp.where(lo, q, 0.0) if h == 0 else jnp.where(lo, 0.0, q)).astype(BF16)
            return lax.dot_general(qh, kcat[p], NT_DIMS, preferred_element_type=F32)

        s_next = scores(*heads[0])
        stats = {}
        for idx, (u, h) in enumerate(heads):
            s = s_next
            if idx + 1 < len(heads):
                s_next = scores(*heads[idx + 1])
            p = units[u][1]
            t = s + bias_ref[sel, u, h * BLOCK:(h + 1) * BLOCK, :]
            m = jnp.max(t, axis=1, keepdims=True)
            e = jnp.exp2(t - m)
            l = jnp.sum(e, axis=1, keepdims=True)
            pv = jnp.dot(e.astype(BF16), vcat[p], preferred_element_type=F32)
            stats[(u, h)] = (m, l, pv)

        for u, (qoff, _, _, _, sink_lo, sink_hi) in enumerate(units):
            cols = slice(qoff, qoff + LANES)
            (m0, l0, pv0), (m1, l1, pv1) = stats[(u, 0)], stats[(u, 1)]
            m_u = jnp.where(lo, m0, m1)
            l_u = jnp.where(lo, l0, l1)
            if has_sink:
                sink_u = jnp.where(lo, sink_ref[sink_lo] * LOG2E, sink_ref[sink_hi] * LOG2E)
                l_u = l_u + jnp.exp2(sink_u - m_u)
            o_sub = jnp.where(lo, pv0, pv1) * (1.0 / l_u)
            if emit_lse:
                lse_sub = m_u + jnp.log2(l_u)
            jq = qoff // LANES
            if r == 1:
                o_ref[pl.ds(row0, BLOCK), cols] = o_sub.astype(o_ref.dtype)
                if emit_lse:
                    lse_ref[jq, pl.ds(row0, BLOCK), :] = lse_sub
            else:
                rows = pl.ds(beta * (BLOCK * r) + c, BLOCK, stride=r)
                o_scr[jq, rows, :] = o_sub
                if emit_lse:
                    lse_ref[jq, rows, :] = lse_sub
        return carry

    lax.fori_loop(0, n_blocks, block, 0, unroll=unroll)
    if r > 1:
        for jq in range(o_scr.shape[0]):
            o_ref[:, jq * LANES:(jq + 1) * LANES] = o_scr[jq].astype(o_ref.dtype)


def band_attn(q, k, v, sinks, units, r, max_dist, emit_lse, unroll=1):
    bsz, _, tr, qw = q.shape
    t = tr * r
    kw = k.shape[3]
    sr = SUPER // r
    has_sink = sinks is not None
    cur = lambda w: pl.BlockSpec((None, r, sr, w), lambda b, i: (b, 0, i, 0))
    prev = lambda w: pl.BlockSpec((None, r, BLOCK, w),
                                  lambda b, i: (b, 0, jnp.maximum(i * (sr // BLOCK) - 1, 0), 0))
    nat = lambda w: pl.BlockSpec((None, SUPER, w), lambda b, i: (b, i, 0))
    args, in_specs = [], []
    if has_sink:
        args.append(sinks)
        in_specs.append(pl.BlockSpec(memory_space=pltpu.SMEM))
    args += [q, k, k, v, v]
    in_specs += [cur(qw), cur(kw), prev(kw), cur(kw), prev(kw)]
    out_shape = [jax.ShapeDtypeStruct((bsz, t, qw), BF16)]
    out_specs = [nat(qw)]
    if emit_lse:
        out_shape.append(jax.ShapeDtypeStruct((bsz, qw // LANES, t, LANES), F32))
        out_specs.append(pl.BlockSpec((None, qw // LANES, SUPER, LANES), lambda b, i: (b, 0, i, 0)))
    scratch = [pltpu.VMEM((2, len(units), 2 * BLOCK, 2 * BLOCK), F32)]
    if r > 1:
        scratch.append(pltpu.VMEM((qw // LANES, SUPER, LANES), F32))
    res = pl.pallas_call(
        functools.partial(_band_attn_kernel, units=tuple(units), n_kv_pairs=kw // LANES,
                          has_sink=has_sink, emit_lse=emit_lse, r=r, max_dist=max_dist,
                          unroll=unroll),
        grid=(bsz, t // SUPER),
        in_specs=in_specs,
        out_specs=out_specs,
        out_shape=out_shape,
        scratch_shapes=scratch,
        compiler_params=_params(2),
        name="band_attn",
    )(*args)
    return res if emit_lse else res[0]


def _units_a():
    units = []
    for g in range(A_GROUP):
        for p in range(A_KV_HEADS // 2):
            h_lo = (2 * p) * A_GROUP + g
            h_hi = (2 * p + 1) * A_GROUP + g
            units.append((g * A_KV_W + p * LANES, p, SLOPES_A[h_lo], SLOPES_A[h_hi], h_lo, h_hi))
    return units


def _units_b(gi):
    units = []
    for p in range(B_SLOTS // 2):
        units.append((p * LANES, p, SLOPES_B[gi * B_SLOTS + 2 * p], SLOPES_B[gi * B_SLOTS + 2 * p + 1],
                      0, 0))
    return units


def _lane_group_mask(rows, width, group):
    lane = lax.broadcasted_iota(jnp.int32, (rows, width), 1)
    return (lane >= group * HEAD_DIM) & (lane < (group + 1) * HEAD_DIM)


def _block_diag_rows(q, n_heads):
    s, width = q.shape
    return jnp.concatenate(
        [jnp.where(_lane_group_mask(s, width, h), q, 0.0) for h in range(n_heads)], axis=0)


def _diag_rows(o_bd, n_heads):
    s = o_bd.shape[0] // n_heads
    width = o_bd.shape[1]
    acc = None
    for h in range(n_heads):
        term = jnp.where(_lane_group_mask(s, width, h), o_bd[h * s:(h + 1) * s], 0.0)
        acc = term if acc is None else acc + term
    return acc


def _pad_new(new_rows):
    s, c = new_rows.shape
    return jnp.concatenate([jnp.zeros((LANES - s, c), F32), new_rows], axis=0)


def _shifted(cache_t, new_pad, s):
    c, w = cache_t.shape
    rolled = pltpu.roll(cache_t, w - s, 1)
    lane_new = lax.broadcasted_iota(jnp.int32, (c, LANES), 1) >= LANES - s
    last = jnp.where(lane_new, new_pad.T, rolled[:, w - LANES:])
    if w == LANES:
        return last
    return jnp.concatenate([rolled[:, :w - LANES], last], axis=1)


def _sample_a_kernel(sink_ref, q_ref, kn_ref, vn_ref, kt_ref, vt_ref, o_ref, ko_ref, vo_ref,
                     bias_c_ref, bias_n_ref, *, bb, s):
    n_rows = A_HEADS * s
    w = kt_ref.shape[-1]
    row1 = lax.broadcasted_iota(jnp.int32, (n_rows, 1), 0)

    @pl.when(pl.program_id(0) == 0)
    def _():
        row = lax.broadcasted_iota(jnp.int32, (n_rows, w), 0)
        tok = row & (s - 1)
        j = lax.broadcasted_iota(jnp.int32, (n_rows, w), 1)
        slope = jnp.zeros((n_rows, w), F32)
        for g in range(A_GROUP):
            for kh in range(A_KV_HEADS):
                slope = jnp.where(row >= (g * A_KV_HEADS + kh) * s,
                                  SLOPES_A[kh * A_GROUP + g] * LOG2E, slope)
        dist_c = w + tok - j
        bias_c_ref[...] = jnp.where((dist_c >= 0) & (dist_c < A_WINDOW),
                                    -(slope * dist_c.astype(F32)), NEG)
        dist_n = tok - (j - (LANES - s))
        bias_n_ref[...] = jnp.where((j >= LANES - s) & (dist_n >= 0) & (dist_n < A_WINDOW),
                                    -(slope * dist_n.astype(F32)), NEG)

    sink = jnp.zeros((n_rows, 1), F32)
    for g in range(A_GROUP):
        for kh in range(A_KV_HEADS):
            sink = jnp.where(row1 >= (g * A_KV_HEADS + kh) * s,
                             sink_ref[kh * A_GROUP + g] * LOG2E, sink)
    bias_c = bias_c_ref[...]
    bias_n = bias_n_ref[...]

    outs = []
    for b in range(bb):
        rows = slice(b * s, (b + 1) * s)
        q = q_ref[rows, :]
        q_bd = jnp.concatenate(
            [_block_diag_rows(q[:, g * A_KV_W:(g + 1) * A_KV_W], A_KV_HEADS) for g in range(A_GROUP)],
            axis=0).astype(BF16)
        kt = kt_ref[b].reshape(A_KV_W, w)
        vt = vt_ref[b].reshape(A_KV_W, w)
        kn = _pad_new(kn_ref[rows, :])
        vn = _pad_new(vn_ref[rows, :])
        tc = jnp.dot(q_bd, kt.astype(BF16), preferred_element_type=F32) + bias_c
        tn = lax.dot_general(q_bd, kn.astype(BF16), NT_DIMS, preferred_element_type=F32) + bias_n
        m = jnp.maximum(jnp.max(tc, axis=1, keepdims=True), jnp.max(tn, axis=1, keepdims=True))
        ec = jnp.exp2(tc - m)
        en = jnp.exp2(tn - m)
        l = (jnp.sum(ec, axis=1, keepdims=True) + jnp.sum(en, axis=1, keepdims=True)
             + jnp.exp2(sink - m))
        inv = 1.0 / l
        o_bd = (lax.dot_general((ec * inv).astype(BF16), vt.astype(BF16), NT_DIMS,
                                preferred_element_type=F32)
                + jnp.dot((en * inv).astype(BF16), vn.astype(BF16), preferred_element_type=F32))
        grp = A_KV_HEADS * s
        outs.append(jnp.concatenate(
            [_diag_rows(o_bd[g * grp:(g + 1) * grp], A_KV_HEADS) for g in range(A_GROUP)], axis=1))
        ko_ref[b] = _shifted(kt, kn, s).reshape(ko_ref.shape[1:])
        vo_ref[b] = _shifted(vt, vn, s).reshape(vo_ref.shape[1:])
    o_ref[...] = jnp.concatenate(outs, axis=0).astype(o_ref.dtype)


def sample_attn_a(sinks, q, kn, vn, kt, vt, l, s, bb=4):
    _, n, kh, d, w = kt.shape
    rows_spec = lambda width: pl.BlockSpec((bb * s, width), lambda i: (i, 0))
    cache = pl.BlockSpec((None, bb, kh, d, w), lambda i: (l, i, 0, 0, 0))
    cache_out = pl.BlockSpec((bb, kh, d, w), lambda i: (i, 0, 0, 0))
    return pl.pallas_call(
        functools.partial(_sample_a_kernel, bb=bb, s=s),
        grid=(n // bb,),
        in_specs=[pl.BlockSpec(memory_space=pltpu.SMEM), rows_spec(A_Q_W), rows_spec(A_KV_W),
                  rows_spec(A_KV_W), cache, cache],
        out_specs=[rows_spec(A_Q_W), cache_out, cache_out],
        out_shape=[jax.ShapeDtypeStruct((n * s, A_Q_W), BF16),
                   jax.ShapeDtypeStruct((n, kh, d, w), F32),
                   jax.ShapeDtypeStruct((n, kh, d, w), F32)],
        scratch_shapes=[pltpu.VMEM((A_HEADS * s, w), F32), pltpu.VMEM((A_HEADS * s, LANES), F32)],
        compiler_params=_params(1),
        name="sample_attn_a",
    )(sinks, q, kn, vn, kt, vt)


def _sample_b_kernel(*refs, s, emit_cache):
    it = iter(refs)
    q_refs = [next(it) for _ in range(B_N_GROUPS)]
    kn_ref, vn_ref, kt_ref, vt_ref, o_ref = (next(it) for _ in range(5))
    ko_ref, vo_ref = (next(it), next(it)) if emit_cache else (None, None)
    bias_refs = [next(it) for _ in range(B_N_GROUPS)]
    bias_n_ref = next(it)
    w = kt_ref.shape[-1]
    n_rows = B_SLOTS * s
    starts = [max(w - (-(-win // LANES)) * LANES, 0) for win, _ in B_PATTERNS]

    def bias_table(gi, n, dist_of_lane):
        win, r = B_PATTERNS[gi]
        assert r & (r - 1) == 0
        row = lax.broadcasted_iota(jnp.int32, (n_rows, n), 0)
        lane = lax.broadcasted_iota(jnp.int32, (n_rows, n), 1)
        slope = jnp.zeros((n_rows, n), F32)
        for slot in range(B_SLOTS):
            slope = jnp.where(row >= slot * s, SLOPES_B[gi * B_SLOTS + slot] * LOG2E, slope)
        dist, ok = dist_of_lane(row & (s - 1), lane)
        valid = ok & (dist >= 0) & (dist <= win) & ((dist & (r - 1)) == 0)
        return jnp.where(valid, -(slope * dist.astype(F32)), NEG)

    @pl.when(pl.program_id(0) == 0)
    def _():
        for gi in range(B_N_GROUPS):
            bias_refs[gi][...] = bias_table(
                gi, w - starts[gi], lambda tok, lane: (w + tok - (lane + starts[gi]), lane >= 0))
            bias_n_ref[gi] = bias_table(
                gi, LANES, lambda tok, lane: (tok - (lane - (LANES - s)), lane >= LANES - s))

    kt = kt_ref[0].reshape(B_W, w)
    vt = vt_ref[0].reshape(B_W, w)
    kn = _pad_new(kn_ref[...])
    vn = _pad_new(vn_ref[...])
    kt16 = kt.astype(BF16)
    kn16 = kn.astype(BF16)
    es, ens, ls, lses = [], [], [], []
    for gi in range(B_N_GROUPS):
        q_bd = _block_diag_rows(q_refs[gi][...], B_SLOTS).astype(BF16)
        tc = jnp.dot(q_bd, kt16[:, starts[gi]:], preferred_element_type=F32) + bias_refs[gi][...]
        tn = lax.dot_general(q_bd, kn16, NT_DIMS, preferred_element_type=F32) + bias_n_ref[gi]
        m = jnp.maximum(jnp.max(tc, axis=1, keepdims=True), jnp.max(tn, axis=1, keepdims=True))
        ec = jnp.exp2(tc - m)
        en = jnp.exp2(tn - m)
        l = jnp.sum(ec, axis=1, keepdims=True) + jnp.sum(en, axis=1, keepdims=True)
        es.append(ec)
        ens.append(en)
        ls.append(l)
        lses.append(m + jnp.log2(l))
    mx = jnp.maximum(jnp.maximum(lses[0], lses[1]), lses[2])
    ws = [jnp.exp2(x - mx) for x in lses]
    den = ws[0] + ws[1] + ws[2]
    cs = [ws[gi] / (den * ls[gi]) for gi in range(B_N_GROUPS)]
    tiles = []
    for tix in range(w // LANES):
        acc = None
        for gi in range(B_N_GROUPS):
            off = tix * LANES - starts[gi]
            if off >= 0:
                term = cs[gi] * es[gi][:, off:off + LANES]
                acc = term if acc is None else acc + term
        tiles.append(acc)
    pc = jnp.concatenate(tiles, axis=1).astype(BF16)
    pn = (cs[0] * ens[0] + cs[1] * ens[1] + cs[2] * ens[2]).astype(BF16)
    o_bd = (lax.dot_general(pc, vt.astype(BF16), NT_DIMS, preferred_element_type=F32)
            + jnp.dot(pn, vn.astype(BF16), preferred_element_type=F32))
    o_ref[...] = _diag_rows(o_bd, B_SLOTS)
    if emit_cache:
        ko_ref[0] = _shifted(kt, kn, s).reshape(ko_ref.shape[1:])
        vo_ref[0] = _shifted(vt, vn, s).reshape(vo_ref.shape[1:])


def sample_attn_b(qs, kn, vn, kt, vt, s, emit_cache):
    n, slots, d, w = kt.shape
    rows_spec = pl.BlockSpec((s, B_W), lambda i: (i, 0))
    cache = pl.BlockSpec((1, slots, d, w), lambda i: (i, 0, 0, 0))
    out_specs = [rows_spec]
    out_shape = [jax.ShapeDtypeStruct((n * s, B_W), F32)]
    if emit_cache:
        out_specs += [cache, cache]
        out_shape += [jax.ShapeDtypeStruct((n, slots, d, w), F32)] * 2
    starts = [max(w - (-(-win // LANES)) * LANES, 0) for win, _ in B_PATTERNS]
    scratch = [pltpu.VMEM((slots * s, w - st), F32) for st in starts]
    scratch.append(pltpu.VMEM((B_N_GROUPS, slots * s, LANES), F32))
    return pl.pallas_call(
        functools.partial(_sample_b_kernel, s=s, emit_cache=emit_cache),
        grid=(n,),
        in_specs=[rows_spec] * (B_N_GROUPS + 2) + [cache, cache],
        out_specs=out_specs,
        out_shape=out_shape,
        scratch_shapes=scratch,
        compiler_params=_params(1),
        name="sample_attn_b",
    )(*qs, kn, vn, kt, vt)


def kernel(x_prompt, x_sample, cache_a_k, cache_a_v, cache_b_k, cache_b_v, g_attn, g_ffn, w_qkv_a,
           sinks_a, w_o_a, g_kv, w_kv_s, w_q_b, w_o_b, w_gate_up, w_down, g_final):
    bp, tp, d = x_prompt.shape
    bs, ts, _ = x_sample.shape
    n_a = w_qkv_a.shape[0]
    depth = g_attn.shape[0]
    wa = cache_a_k.shape[2]
    wb = cache_b_k.shape[1]
    wa_p = min(A_WINDOW, tp)
    wb_p = min(max(w for w, _ in B_PATTERNS), tp)
    assert wa == LANES and wb % LANES == 0 and tp % SUPER == 0

    xp = x_prompt
    xs = x_sample.reshape(1, bs * ts, d)

    wq = w_qkv_a[:, :, :A_Q_W].reshape(n_a, d, A_KV_HEADS, A_GROUP, HEAD_DIM)
    wq = wq.transpose(0, 1, 3, 2, 4).reshape(n_a, d, A_Q_W) * QSCALE
    w_qkv = jnp.concatenate([wq, w_qkv_a[:, :, A_Q_W:]], axis=2).astype(BF16)
    w_oa = w_o_a.reshape(n_a, A_KV_HEADS, A_GROUP, HEAD_DIM, d).transpose(0, 2, 1, 3, 4)
    w_oa = w_oa.reshape(n_a, A_Q_W, d).astype(BF16)
    w_kv = w_kv_s.astype(BF16)[None]
    w_qb = (w_q_b * QSCALE).astype(BF16)
    w_ob = w_o_b.astype(BF16)
    w_gu = w_gate_up.astype(BF16)
    w_dn = w_down.astype(BF16)

    cak_t = cache_a_k.transpose(0, 1, 3, 4, 2)
    cav_t = cache_a_v.transpose(0, 1, 3, 4, 2)
    cbk_t = cache_b_k.transpose(0, 2, 3, 1)
    cbv_t = cache_b_v.transpose(0, 2, 3, 1)

    units_a = _units_a()
    a_k_p, a_v_p, a_k_s, a_v_s = [], [], [], []
    qkv_outs = ((0, A_Q_W, 1, BF16), (A_Q_W, A_KV_W, 1, F32), (A_Q_W + A_KV_W, A_KV_W, 1, F32))
    qkv_outs_s = ((0, A_Q_W, 1, F32),) + qkv_outs[1:]
    qb_outs = tuple((gi * B_W, B_W, r, BF16) for gi, (_, r) in enumerate(B_PATTERNS))
    qb_outs_s = tuple((gi * B_W, B_W, 1, F32) for gi in range(B_N_GROUPS))
    kv_outs = [(0, B_W, -wb_p, F32), (B_W, B_W, -wb_p, F32)]
    for _, r in B_PATTERNS:
        kv_outs += [(0, B_W, r, BF16), (B_W, B_W, r, BF16)]
    kv_outs_s = [(0, B_W, 1, F32), (B_W, B_W, 1, F32)]

    def flat(x):
        return x.reshape(-1, x.shape[-1])

    for l in range(depth):
        last = l == depth - 1
        gfin = g_final if last else None
        if l < n_a:
            qp, kp, vp = norm_mm(xp, g_attn[l], w_qkv, l, qkv_outs)
            qs, ks_, vs_ = norm_mm(xs, g_attn[l], w_qkv, l, qkv_outs_s)
            op = band_attn(qp[:, None], kp[:, None], vp[:, None], sinks_a[l], units_a, 1,
                           A_WINDOW - 1, False, unroll=8)
            osm, ko, vo = sample_attn_a(sinks_a[l], flat(qs), flat(ks_), flat(vs_), cak_t, cav_t, l, ts)
            a_k_p.append(kp[:, -wa_p:].reshape(bp, wa_p, A_KV_HEADS, HEAD_DIM))
            a_v_p.append(vp[:, -wa_p:].reshape(bp, wa_p, A_KV_HEADS, HEAD_DIM))
            a_k_s.append(ko.transpose(0, 3, 1, 2))
            a_v_s.append(vo.transpose(0, 3, 1, 2))
            xp = proj_ffn(flat(xp), flat(op), None, w_oa, l, g_ffn[l], w_gu, w_dn, l, gfin)
            xs = proj_ffn(flat(xs), osm, None, w_oa, l, g_ffn[l], w_gu, w_dn, l, gfin)
            xp = xp.reshape(bp, tp, d)
            xs = xs.reshape(1, bs * ts, d)
        else:
            b = l - n_a
            if l == n_a:
                kv = norm_mm(xp, g_kv, w_kv, 0, kv_outs)
                kbp, vbp = kv[0], kv[1]
                k_sub, v_sub = kv[2::2], kv[3::2]
                kbs, vbs = norm_mm(xs, g_kv, w_kv, 0, kv_outs_s)
                kbs, vbs = flat(kbs), flat(vbs)
            q_groups = norm_mm(xp, g_attn[l], w_qb, b, qb_outs)
            outs, lses = [], []
            for gi, (w, r) in enumerate(B_PATTERNS):
                q_sub = q_groups[gi] if r > 1 else q_groups[gi][:, None]
                k_g = k_sub[gi] if r > 1 else k_sub[gi][:, None]
                v_g = v_sub[gi] if r > 1 else v_sub[gi][:, None]
                o_g, lse_g = band_attn(q_sub, k_g, v_g, None, _units_b(gi), r, w // r, True, unroll=8)
                outs.append(flat(o_g))
                lses.append(lse_g)
            qs_groups = [flat(q) for q in norm_mm(xs, g_attn[l], w_qb, b, qb_outs_s)]
            if l == n_a:
                osm, kcb_t, vcb_t = sample_attn_b(qs_groups, kbs, vbs, cbk_t, cbv_t, ts, True)
            else:
                osm = sample_attn_b(qs_groups, kbs, vbs, cbk_t, cbv_t, ts, False)[0]
            xp = proj_ffn(flat(xp), outs, lses, w_ob, b, g_ffn[l], w_gu, w_dn, l, gfin)
            xs = proj_ffn(flat(xs), osm, None, w_ob, b, g_ffn[l], w_gu, w_dn, l, gfin)
            xp = xp.reshape(bp, tp, d)
            xs = xs.reshape(1, bs * ts, d)

    y_sample = xs.reshape(bs, ts, d)
    b_k_p = kbp.reshape(bp, wb_p, B_SLOTS, HEAD_DIM)
    b_v_p = vbp.reshape(bp, wb_p, B_SLOTS, HEAD_DIM)
    return (xp, y_sample,
            jnp.stack(a_k_p, axis=0), jnp.stack(a_v_p, axis=0), b_k_p, b_v_p,
            jnp.stack(a_k_s, axis=0), jnp.stack(a_v_s, axis=0),
            kcb_t.transpose(0, 3, 1, 2), vcb_t.transpose(0, 3, 1, 2))
```

```python
import functools
import math

import jax
import jax.numpy as jnp
from jax import lax
from jax.experimental import pallas as pl
from jax.experimental.pallas import tpu as pltpu

D_MODEL = 1024
HEAD_DIM = 64
A_HEADS = 16
A_KV_HEADS = 4
A_GROUP = A_HEADS // A_KV_HEADS
A_WINDOW = 128
A_Q_W = A_HEADS * HEAD_DIM
A_KV_W = A_KV_HEADS * HEAD_DIM
B_SLOTS = 8
B_PATTERNS = ((128, 1), (512, 4), (2048, 16))
B_N_GROUPS = len(B_PATTERNS)
B_HEADS = B_N_GROUPS * B_SLOTS
B_W = B_SLOTS * HEAD_DIM
BLOCK = 128
EPS = 1e-6
NEG = -1e30
LOG2E = math.log2(math.e)
LN2 = math.log(2.0)
QSCALE = HEAD_DIM ** -0.5 * LOG2E

LANES = 128
HALF = LANES // 2
SUPER = 16 * BLOCK
MAX_ROW_STRIDE = 4
VMEM_LIMIT = 56 * 1024 * 1024

F32 = jnp.float32
BF16 = jnp.bfloat16
NT_DIMS = (((1,), (1,)), ((), ()))


def _slopes(n):
    return [float(2.0 ** (-8.0 * (i + 1) / n)) for i in range(n)]


SLOPES_A = _slopes(A_HEADS)
SLOPES_B = _slopes(B_HEADS)


def _rms(x, g):
    ms = jnp.mean(x * x, axis=-1, keepdims=True)
    return (x * lax.rsqrt(ms + EPS)) * g


def _params(n_grid):
    return pltpu.CompilerParams(dimension_semantics=("arbitrary",) * n_grid,
                                vmem_limit_bytes=VMEM_LIMIT)


def _const_spec(shape):
    nd = len(shape)
    return pl.BlockSpec(shape, lambda *_: (0,) * nd, pipeline_mode=pl.Buffered(1))


def _layer_spec(w, l):
    return pl.BlockSpec((None,) + w.shape[1:], lambda *_: (l, 0, 0), pipeline_mode=pl.Buffered(1))


def _norm_mm_kernel(x_ref, g_ref, w_ref, *refs, outs):
    out_refs, scr_refs = refs[:len(outs)], refs[len(outs):]
    h = _rms(x_ref[...], g_ref[...])
    y = jnp.dot(h.astype(BF16), w_ref[...], preferred_element_type=F32)
    si = 0
    for o_ref, (off, width, r, _) in zip(out_refs, outs):
        yo = y[:, off:off + width]
        if r <= 1:
            o_ref[...] = yo.astype(o_ref.dtype)
        else:
            scr, scr_b = scr_refs[si], scr_refs[si + 1]
            si += 2
            tm = yo.shape[0]
            for j in range(width // LANES):
                cols = slice(j * LANES, (j + 1) * LANES)
                scr[j] = yo[:, cols]
                if r <= MAX_ROW_STRIDE:
                    for c in range(r):
                        o_ref[c, :, cols] = scr[j, pl.ds(c, tm // r, stride=r), :].astype(o_ref.dtype)
                else:
                    r1, r2 = MAX_ROW_STRIDE, r // MAX_ROW_STRIDE
                    q = tm // r1
                    for c1 in range(r1):
                        scr_b[j, c1 * q:(c1 + 1) * q, :] = scr[j, pl.ds(c1, q, stride=r1), :]
                    for c1 in range(r1):
                        for c2 in range(r2):
                            o_ref[c2 * r1 + c1, :, cols] = scr_b[
                                j, pl.ds(c1 * q + c2, tm // r, stride=r2), :].astype(o_ref.dtype)


def norm_mm(x, g, w, l, outs, tm=512):
    bsz, t, d = x.shape
    assert t % tm == 0
    in_specs = [pl.BlockSpec((None, tm, d), lambda b, j: (b, j, 0)), _const_spec((1, d)),
                _layer_spec(w, l)]
    out_specs, out_shape, scratch = [], [], []
    for (_, width, r, dt) in outs:
        if r == 1:
            out_specs.append(pl.BlockSpec((None, tm, width), lambda b, j: (b, j, 0)))
            out_shape.append(jax.ShapeDtypeStruct((bsz, t, width), dt))
        elif r < 0:
            tail = -r
            assert tail % tm == 0 and tail <= t
            first = (t - tail) // tm
            out_specs.append(pl.BlockSpec((None, tm, width),
                                          lambda b, j, first=first: (b, jnp.maximum(j - first, 0), 0)))
            out_shape.append(jax.ShapeDtypeStruct((bsz, tail, width), dt))
        else:
            assert (tm // r) % 16 == 0
            out_specs.append(pl.BlockSpec((None, r, tm // r, width), lambda b, j: (b, 0, j, 0)))
            out_shape.append(jax.ShapeDtypeStruct((bsz, r, t // r, width), dt))
            assert r <= MAX_ROW_STRIDE or (r % MAX_ROW_STRIDE == 0 and r // MAX_ROW_STRIDE <= MAX_ROW_STRIDE)
            scratch += [pltpu.VMEM((width // LANES, tm, LANES), F32)] * 2
    return pl.pallas_call(
        functools.partial(_norm_mm_kernel, outs=tuple(outs)),
        grid=(bsz, t // tm),
        in_specs=in_specs,
        out_specs=out_specs,
        out_shape=out_shape,
        scratch_shapes=scratch,
        compiler_params=_params(2),
        name="norm_mm",
    )(x, g.reshape(1, d), w)


def _proj_ffn_kernel(*refs, combine, final, d_ff):
    it = iter(refs)
    x_ref = next(it)
    if combine:
        o_refs = [next(it) for _ in range(B_N_GROUPS)]
        l_refs = [next(it) for _ in range(B_N_GROUPS)]
    else:
        o_ref = next(it)
    wo_ref, g_ref, wgu_ref, wdn_ref = next(it), next(it), next(it), next(it)
    gfin_ref = next(it) if final else None
    out_ref = next(it)

    if combine:
        ls = [jnp.concatenate([r[j] for j in range(r.shape[0])], axis=1) for r in l_refs]
        mx = jnp.maximum(jnp.maximum(ls[0], ls[1]), ls[2])
        es = [jnp.exp2(l - mx) for l in ls]
        den = es[0] + es[1] + es[2]
        num = es[0] * o_refs[0][...].astype(F32)
        num = num + es[1] * o_refs[1][...].astype(F32)
        num = num + es[2] * o_refs[2][...].astype(F32)
        o = (num / den).astype(BF16)
    else:
        o = o_ref[...].astype(BF16)
    x = x_ref[...] + jnp.dot(o, wo_ref[...], preferred_element_type=F32)
    h = _rms(x, g_ref[...]).astype(BF16)
    gu = jnp.dot(h, wgu_ref[...], preferred_element_type=F32)
    gate = gu[:, :d_ff]
    up = gu[:, d_ff:]
    act = (gate / (1.0 + jnp.exp(-gate))) * up
    y = x + jnp.dot(act.astype(BF16), wdn_ref[...], preferred_element_type=F32)
    if final:
        y = _rms(y, gfin_ref[...])
    out_ref[...] = y


def proj_ffn(x, o, lses, w_o, lo, g, w_gu, w_dn, l, g_final=None, tm=256):
    rows, d = x.shape
    combine = lses is not None
    final = g_final is not None
    d_ff = w_dn.shape[1]
    ow = w_o.shape[1]
    row_spec = lambda w: pl.BlockSpec((tm, w), lambda i: (i, 0))
    args, in_specs = [x], [row_spec(d)]
    if combine:
        args += list(o) + list(lses)
        tpb = lses[0].shape[2] // tm
        in_specs += [row_spec(ow)] * B_N_GROUPS
        in_specs += [pl.BlockSpec((None, ow // LANES, tm, LANES),
                                  lambda i: (i // tpb, 0, i % tpb, 0))] * B_N_GROUPS
    else:
        args.append(o)
        in_specs.append(row_spec(ow))
    args += [w_o, g.reshape(1, d), w_gu, w_dn]
    in_specs += [_layer_spec(w_o, lo), _const_spec((1, d)), _layer_spec(w_gu, l), _layer_spec(w_dn, l)]
    if final:
        args.append(g_final.reshape(1, d))
        in_specs.append(_const_spec((1, d)))
    return pl.pallas_call(
        functools.partial(_proj_ffn_kernel, combine=combine, final=final, d_ff=d_ff),
        grid=(rows // tm,),
        in_specs=in_specs,
        out_specs=row_spec(d),
        out_shape=jax.ShapeDtypeStruct((rows, d), F32),
        compiler_params=_params(1),
        name="proj_ffn",
    )(*args)


def _band_attn_kernel(*refs, units, n_kv_pairs, has_sink, emit_lse, r, max_dist, unroll):
    it = iter(refs)
    sink_ref = next(it) if has_sink else None
    q_ref, kc_ref, kp_ref, vc_ref, vp_ref = (next(it) for _ in range(5))
    o_ref = next(it)
    lse_ref = next(it) if emit_lse else None
    bias_ref = next(it)
    o_scr = next(it) if r > 1 else None
    n_blocks = SUPER // BLOCK
    nbeta = n_blocks // r
    step = pl.program_id(1)

    @pl.when((pl.program_id(0) == 0) & (step == 0))
    def _():
        qi = lax.broadcasted_iota(jnp.int32, (2 * BLOCK, 2 * BLOCK), 0) & (BLOCK - 1)
        kj = lax.broadcasted_iota(jnp.int32, (2 * BLOCK, 2 * BLOCK), 1)
        row = lax.broadcasted_iota(jnp.int32, (2 * BLOCK, 2 * BLOCK), 0)
        dist = BLOCK + qi - kj
        valid = (dist >= 0) & (dist <= max_dist)
        distf = (r * dist).astype(F32)
        for u, (_, _, slope_lo, slope_hi, _, _) in enumerate(units):
            slope = jnp.where(row < BLOCK, slope_lo * LOG2E, slope_hi * LOG2E)
            b = -(slope * distf)
            bias_ref[0, u] = jnp.where(valid, b, NEG)
            bias_ref[1, u] = jnp.where(valid & (kj >= BLOCK), b, NEG)

    lane = lax.broadcasted_iota(jnp.int32, (BLOCK, LANES), 1)
    lo = lane < HALF
    row1 = lax.broadcasted_iota(jnp.int32, (2 * BLOCK, 1), 0)

    def block(blk, carry):
        if nbeta == 1:
            c, beta = blk, 0
        elif r == 1:
            c, beta = 0, blk
        else:
            c = lax.shift_right_logical(blk, nbeta.bit_length() - 1)
            beta = blk & (nbeta - 1)
        if nbeta == 1:
            row0 = prow0 = 0
        else:
            row0 = pl.multiple_of(beta * BLOCK, BLOCK)
            prow0 = pl.multiple_of(jnp.maximum(beta - 1, 0) * BLOCK, BLOCK)
        first_beta = beta == 0
        sel = jnp.where(first_beta & (step == 0), 1, 0)

        def prev_cur(cur_ref, prev_ref, sl):
            cur = cur_ref[c, pl.ds(row0, BLOCK), sl]
            if nbeta == 1:
                prev = prev_ref[c, :, sl]
            else:
                prev = jnp.where(first_beta, prev_ref[c, :, sl], cur_ref[c, pl.ds(prow0, BLOCK), sl])
            return jnp.concatenate([prev, cur], axis=0).astype(BF16)

        kcat, vcat = [], []
        for p in range(n_kv_pairs):
            sl = slice(p * LANES, (p + 1) * LANES)
            kcat.append(prev_cur(kc_ref, kp_ref, sl))
            vcat.append(prev_cur(vc_ref, vp_ref, sl))

        heads = [(u, h) for u in range(len(units)) for h in range(2)]

        def scores(u, h):
            qoff, p = units[u][0], units[u][1]
            q = q_ref[c, pl.ds(row0, BLOCK), qoff:qoff + LANES].astype(F32)
            qh = (jnp.where(lo, q, 0.0) if h == 0 else jnp.where(lo, 0.0, q)).astype(BF16)
            return lax.dot_general(qh, kcat[p], NT_DIMS, preferred_element_type=F32)

        s_next = scores(*heads[0])
        stats = {}
        for idx, (u, h) in enumerate(heads):
            s = s_next
            if idx + 1 < len(heads):
                s_next = scores(*heads[idx + 1])
            p = units[u][1]
            t = s + bias_ref[sel, u, h * BLOCK:(h + 1) * BLOCK, :]
            m = jnp.max(t, axis=1, keepdims=True)
            e = jnp.exp2(t - m)
            l = jnp.sum(e, axis=1, keepdims=True)
            pv = jnp.dot(e.astype(BF16), vcat[p], preferred_element_type=F32)
            stats[(u, h)] = (m, l, pv)

        for u, (qoff, _, _, _, sink_lo, sink_hi) in enumerate(units):
            cols = slice(qoff, qoff + LANES)
            (m0, l0, pv0), (m1, l1, pv1) = stats[(u, 0)], stats[(u, 1)]
            m_u = jnp.where(lo, m0, m1)
            l_u = jnp.where(lo, l0, l1)
            if has_sink:
                sink_u = jnp.where(lo, sink_ref[sink_lo] * LOG2E, sink_ref[sink_hi] * LOG2E)
                l_u = l_u + jnp.exp2(sink_u - m_u)
            o_sub = jnp.where(lo, pv0, pv1) * (1.0 / l_u)
            if emit_lse:
                lse_sub = m_u + jnp.log2(l_u)
            jq = qoff // LANES
            if r == 1:
                o_ref[pl.ds(row0, BLOCK), cols] = o_sub.astype(o_ref.dtype)
                if emit_lse:
                    lse_ref[jq, pl.ds(row0, BLOCK), :] = lse_sub
            else:
                rows = pl.ds(beta * (BLOCK * r) + c, BLOCK, stride=r)
                o_scr[jq, rows, :] = o_sub
                if emit_lse:
                    lse_ref[jq, rows, :] = lse_sub
        return carry

    lax.fori_loop(0, n_blocks, block, 0, unroll=unroll)
    if r > 1:
        for jq in range(o_scr.shape[0]):
            o_ref[:, jq * LANES:(jq + 1) * LANES] = o_scr[jq].astype(o_ref.dtype)


def band_attn(q, k, v, sinks, units, r, max_dist, emit_lse, unroll=1):
    bsz, _, tr, qw = q.shape
    t = tr * r
    kw = k.shape[3]
    sr = SUPER // r
    has_sink = sinks is not None
    cur = lambda w: pl.BlockSpec((None, r, sr, w), lambda b, i: (b, 0, i, 0))
    prev = lambda w: pl.BlockSpec((None, r, BLOCK, w),
                                  lambda b, i: (b, 0, jnp.maximum(i * (sr // BLOCK) - 1, 0), 0))
    nat = lambda w: pl.BlockSpec((None, SUPER, w), lambda b, i: (b, i, 0))
    args, in_specs = [], []
    if has_sink:
        args.append(sinks)
        in_specs.append(pl.BlockSpec(memory_space=pltpu.SMEM))
    args += [q, k, k, v, v]
    in_specs += [cur(qw), cur(kw), prev(kw), cur(kw), prev(kw)]
    out_shape = [jax.ShapeDtypeStruct((bsz, t, qw), BF16)]
    out_specs = [nat(qw)]
    if emit_lse:
        out_shape.append(jax.ShapeDtypeStruct((bsz, qw // LANES, t, LANES), F32))
        out_specs.append(pl.BlockSpec((None, qw // LANES, SUPER, LANES), lambda b, i: (b, 0, i, 0)))
    scratch = [pltpu.VMEM((2, len(units), 2 * BLOCK, 2 * BLOCK), F32)]
    if r > 1:
        scratch.append(pltpu.VMEM((qw // LANES, SUPER, LANES), F32))
    res = pl.pallas_call(
        functools.partial(_band_attn_kernel, units=tuple(units), n_kv_pairs=kw // LANES,
                          has_sink=has_sink, emit_lse=emit_lse, r=r, max_dist=max_dist,
                          unroll=unroll),
        grid=(bsz, t // SUPER),
        in_specs=in_specs,
        out_specs=out_specs,
        out_shape=out_shape,
        scratch_shapes=scratch,
        compiler_params=_params(2),
        name="band_attn",
    )(*args)
    return res if emit_lse else res[0]


def _units_a():
    units = []
    for g in range(A_GROUP):
        for p in range(A_KV_HEADS // 2):
            h_lo = (2 * p) * A_GROUP + g
            h_hi = (2 * p + 1) * A_GROUP + g
            units.append((g * A_KV_W + p * LANES, p, SLOPES_A[h_lo], SLOPES_A[h_hi], h_lo, h_hi))
    return units


def _units_b(gi):
    units = []
    for p in range(B_SLOTS // 2):
        units.append((p * LANES, p, SLOPES_B[gi * B_SLOTS + 2 * p], SLOPES_B[gi * B_SLOTS + 2 * p + 1],
                      0, 0))
    return units


def _lane_group_mask(rows, width, group):
    lane = lax.broadcasted_iota(jnp.int32, (rows, width), 1)
    return (lane >= group * HEAD_DIM) & (lane < (group + 1) * HEAD_DIM)


def _block_diag_rows(q, n_heads):
    s, width = q.shape
    return jnp.concatenate(
        [jnp.where(_lane_group_mask(s, width, h), q, 0.0) for h in range(n_heads)], axis=0)


def _diag_rows(o_bd, n_heads):
    s = o_bd.shape[0] // n_heads
    width = o_bd.shape[1]
    acc = None
    for h in range(n_heads):
        term = jnp.where(_lane_group_mask(s, width, h), o_bd[h * s:(h + 1) * s], 0.0)
        acc = term if acc is None else acc + term
    return acc


def _pad_new(new_rows):
    s, c = new_rows.shape
    return jnp.concatenate([jnp.zeros((LANES - s, c), F32), new_rows], axis=0)


def _shifted(cache_t, new_pad, s):
    c, w = cache_t.shape
    rolled = pltpu.roll(cache_t, w - s, 1)
    lane_new = lax.broadcasted_iota(jnp.int32, (c, LANES), 1) >= LANES - s
    last = jnp.where(lane_new, new_pad.T, rolled[:, w - LANES:])
    if w == LANES:
        return last
    return jnp.concatenate([rolled[:, :w - LANES], last], axis=1)


def _sample_a_kernel(sink_ref, q_ref, kn_ref, vn_ref, kt_ref, vt_ref, o_ref, ko_ref, vo_ref,
                     bias_c_ref, bias_n_ref, *, bb, s):
    n_rows = A_HEADS * s
    w = kt_ref.shape[-1]
    row1 = lax.broadcasted_iota(jnp.int32, (n_rows, 1), 0)

    @pl.when(pl.program_id(0) == 0)
    def _():
        row = lax.broadcasted_iota(jnp.int32, (n_rows, w), 0)
        tok = row & (s - 1)
        j = lax.broadcasted_iota(jnp.int32, (n_rows, w), 1)
        slope = jnp.zeros((n_rows, w), F32)
        for g in range(A_GROUP):
            for kh in range(A_KV_HEADS):
                slope = jnp.where(row >= (g * A_KV_HEADS + kh) * s,
                                  SLOPES_A[kh * A_GROUP + g] * LOG2E, slope)
        dist_c = w + tok - j
        bias_c_ref[...] = jnp.where((dist_c >= 0) & (dist_c < A_WINDOW),
                                    -(slope * dist_c.astype(F32)), NEG)
        dist_n = tok - (j - (LANES - s))
        bias_n_ref[...] = jnp.where((j >= LANES - s) & (dist_n >= 0) & (dist_n < A_WINDOW),
                                    -(slope * dist_n.astype(F32)), NEG)

    sink = jnp.zeros((n_rows, 1), F32)
    for g in range(A_GROUP):
        for kh in range(A_KV_HEADS):
            sink = jnp.where(row1 >= (g * A_KV_HEADS + kh) * s,
                             sink_ref[kh * A_GROUP + g] * LOG2E, sink)
    bias_c = bias_c_ref[...]
    bias_n = bias_n_ref[...]

    outs = []
    for b in range(bb):
        rows = slice(b * s, (b + 1) * s)
        q = q_ref[rows, :]
        q_bd = jnp.concatenate(
            [_block_diag_rows(q[:, g * A_KV_W:(g + 1) * A_KV_W], A_KV_HEADS) for g in range(A_GROUP)],
            axis=0).astype(BF16)
        kt = kt_ref[b].reshape(A_KV_W, w)
        vt = vt_ref[b].reshape(A_KV_W, w)
        kn = _pad_new(kn_ref[rows, :])
        vn = _pad_new(vn_ref[rows, :])
        tc = jnp.dot(q_bd, kt.astype(BF16), preferred_element_type=F32) + bias_c
        tn = lax.dot_general(q_bd, kn.astype(BF16), NT_DIMS, preferred_element_type=F32) + bias_n
        m = jnp.maximum(jnp.max(tc, axis=1, keepdims=True), jnp.max(tn, axis=1, keepdims=True))
        ec = jnp.exp2(tc - m)
        en = jnp.exp2(tn - m)
        l = (jnp.sum(ec, axis=1, keepdims=True) + jnp.sum(en, axis=1, keepdims=True)
             + jnp.exp2(sink - m))
        inv = 1.0 / l
        o_bd = (lax.dot_general((ec * inv).astype(BF16), vt.astype(BF16), NT_DIMS,
                                preferred_element_type=F32)
                + jnp.dot((en * inv).astype(BF16), vn.astype(BF16), preferred_element_type=F32))
        grp = A_KV_HEADS * s
        outs.append(jnp.concatenate(
            [_diag_rows(o_bd[g * grp:(g + 1) * grp], A_KV_HEADS) for g in range(A_GROUP)], axis=1))
        ko_ref[b] = _shifted(kt, kn, s).reshape(ko_ref.shape[1:])
        vo_ref[b] = _shifted(vt, vn, s).reshape(vo_ref.shape[1:])
    o_ref[...] = jnp.concatenate(outs, axis=0).astype(o_ref.dtype)


def sample_attn_a(sinks, q, kn, vn, kt, vt, l, s, bb=4):
    _, n, kh, d, w = kt.shape
    rows_spec = lambda width: pl.BlockSpec((bb * s, width), lambda i: (i, 0))
    cache = pl.BlockSpec((None, bb, kh, d, w), lambda i: (l, i, 0, 0, 0))
    cache_out = pl.BlockSpec((bb, kh, d, w), lambda i: (i, 0, 0, 0))
    return pl.pallas_call(
        functools.partial(_sample_a_kernel, bb=bb, s=s),
        grid=(n // bb,),
        in_specs=[pl.BlockSpec(memory_space=pltpu.SMEM), rows_spec(A_Q_W), rows_spec(A_KV_W),
                  rows_spec(A_KV_W), cache, cache],
        out_specs=[rows_spec(A_Q_W), cache_out, cache_out],
        out_shape=[jax.ShapeDtypeStruct((n * s, A_Q_W), BF16),
                   jax.ShapeDtypeStruct((n, kh, d, w), F32),
                   jax.ShapeDtypeStruct((n, kh, d, w), F32)],
        scratch_shapes=[pltpu.VMEM((A_HEADS * s, w), F32), pltpu.VMEM((A_HEADS * s, LANES), F32)],
        compiler_params=_params(1),
        name="sample_attn_a",
    )(sinks, q, kn, vn, kt, vt)


def _sample_b_kernel(*refs, s, bb, emit_cache):
    it = iter(refs)
    q_refs = [next(it) for _ in range(B_N_GROUPS)]
    kn_ref, vn_ref, kt_ref, vt_ref, o_ref = (next(it) for _ in range(5))
    ko_ref, vo_ref = (next(it), next(it)) if emit_cache else (None, None)
    bias_refs = [next(it) for _ in range(B_N_GROUPS)]
    bias_n_ref = next(it)
    w = kt_ref.shape[-1]
    n_rows = B_SLOTS * s
    starts = [max(w - (-(-win // LANES)) * LANES, 0) for win, _ in B_PATTERNS]

    def bias_table(gi, n, dist_of_lane):
        win, r = B_PATTERNS[gi]
        assert r & (r - 1) == 0
        row = lax.broadcasted_iota(jnp.int32, (n_rows, n), 0)
        lane = lax.broadcasted_iota(jnp.int32, (n_rows, n), 1)
        slope = jnp.zeros((n_rows, n), F32)
        for slot in range(B_SLOTS):
            slope = jnp.where(row >= slot * s, SLOPES_B[gi * B_SLOTS + slot] * LOG2E, slope)
        dist, ok = dist_of_lane(row & (s - 1), lane)
        valid = ok & (dist >= 0) & (dist <= win) & ((dist & (r - 1)) == 0)
        return jnp.where(valid, -(slope * dist.astype(F32)), NEG)

    @pl.when(pl.program_id(0) == 0)
    def _():
        for gi in range(B_N_GROUPS):
            bias_refs[gi][...] = bias_table(
                gi, w - starts[gi], lambda tok, lane: (w + tok - (lane + starts[gi]), lane >= 0))
            bias_n_ref[gi] = bias_table(
                gi, LANES, lambda tok, lane: (tok - (lane - (LANES - s)), lane >= LANES - s))

    for b in range(bb):
        _sample_b_one(b, slice(b * s, (b + 1) * s), q_refs, kn_ref, vn_ref, kt_ref, vt_ref, o_ref,
                      ko_ref, vo_ref, bias_refs, bias_n_ref, starts, s)


def _sample_b_one(b, rows, q_refs, kn_ref, vn_ref, kt_ref, vt_ref, o_ref, ko_ref, vo_ref,
                  bias_refs, bias_n_ref, starts, s):
    w = kt_ref.shape[-1]
    kt = kt_ref[b].reshape(B_W, w)
    vt = vt_ref[b].reshape(B_W, w)
    kn = _pad_new(kn_ref[rows, :])
    vn = _pad_new(vn_ref[rows, :])
    kt16 = kt.astype(BF16)
    kn16 = kn.astype(BF16)
    es, ens, ls, lses = [], [], [], []
    for gi in range(B_N_GROUPS):
        q_bd = _block_diag_rows(q_refs[gi][rows, :], B_SLOTS).astype(BF16)
        tc = jnp.dot(q_bd, kt16[:, starts[gi]:], preferred_element_type=F32) + bias_refs[gi][...]
        tn = lax.dot_general(q_bd, kn16, NT_DIMS, preferred_element_type=F32) + bias_n_ref[gi]
        m = jnp.maximum(jnp.max(tc, axis=1, keepdims=True), jnp.max(tn, axis=1, keepdims=True))
        ec = jnp.exp2(tc - m)
        en = jnp.exp2(tn - m)
        l = jnp.sum(ec, axis=1, keepdims=True) + jnp.sum(en, axis=1, keepdims=True)
        es.append(ec)
        ens.append(en)
        ls.append(l)
        lses.append(m + jnp.log2(l))
    mx = jnp.maximum(jnp.maximum(lses[0], lses[1]), lses[2])
    ws = [jnp.exp2(x - mx) for x in lses]
    den = ws[0] + ws[1] + ws[2]
    cs = [ws[gi] / (den * ls[gi]) for gi in range(B_N_GROUPS)]
    tiles = []
    for tix in range(w // LANES):
        acc = None
        for gi in range(B_N_GROUPS):
            off = tix * LANES - starts[gi]
            if off >= 0:
                term = cs[gi] * es[gi][:, off:off + LANES]
                acc = term if acc is None else acc + term
        tiles.append(acc)
    pc = jnp.concatenate(tiles, axis=1).astype(BF16)
    pn = (cs[0] * ens[0] + cs[1] * ens[1] + cs[2] * ens[2]).astype(BF16)
    o_bd = (lax.dot_general(pc, vt.astype(BF16), NT_DIMS, preferred_element_type=F32)
            + jnp.dot(pn, vn.astype(BF16), preferred_element_type=F32))
    o_ref[rows, :] = _diag_rows(o_bd, B_SLOTS)
    if ko_ref is not None:
        ko_ref[b] = _shifted(kt, kn, s).reshape(ko_ref.shape[1:])
        vo_ref[b] = _shifted(vt, vn, s).reshape(vo_ref.shape[1:])


def sample_attn_b(qs, kn, vn, kt, vt, s, emit_cache, bb=1):
    n, slots, d, w = kt.shape
    rows_spec = pl.BlockSpec((bb * s, B_W), lambda i: (i, 0))
    cache = pl.BlockSpec((bb, slots, d, w), lambda i: (i, 0, 0, 0))
    out_specs = [rows_spec]
    out_shape = [jax.ShapeDtypeStruct((n * s, B_W), F32)]
    if emit_cache:
        out_specs += [cache, cache]
        out_shape += [jax.ShapeDtypeStruct((n, slots, d, w), F32)] * 2
    starts = [max(w - (-(-win // LANES)) * LANES, 0) for win, _ in B_PATTERNS]
    scratch = [pltpu.VMEM((slots * s, w - st), F32) for st in starts]
    scratch.append(pltpu.VMEM((B_N_GROUPS, slots * s, LANES), F32))
    return pl.pallas_call(
        functools.partial(_sample_b_kernel, s=s, bb=bb, emit_cache=emit_cache),
        grid=(n // bb,),
        in_specs=[rows_spec] * (B_N_GROUPS + 2) + [cache, cache],
        out_specs=out_specs,
        out_shape=out_shape,
        scratch_shapes=scratch,
        compiler_params=_params(1),
        name="sample_attn_b",
    )(*qs, kn, vn, kt, vt)


def kernel(x_prompt, x_sample, cache_a_k, cache_a_v, cache_b_k, cache_b_v, g_attn, g_ffn, w_qkv_a,
           sinks_a, w_o_a, g_kv, w_kv_s, w_q_b, w_o_b, w_gate_up, w_down, g_final):
    bp, tp, d = x_prompt.shape
    bs, ts, _ = x_sample.shape
    n_a = w_qkv_a.shape[0]
    depth = g_attn.shape[0]
    wa = cache_a_k.shape[2]
    wb = cache_b_k.shape[1]
    wa_p = min(A_WINDOW, tp)
    wb_p = min(max(w for w, _ in B_PATTERNS), tp)
    assert wa == LANES and wb % LANES == 0 and tp % SUPER == 0

    xp = x_prompt
    xs = x_sample.reshape(1, bs * ts, d)

    wq = w_qkv_a[:, :, :A_Q_W].reshape(n_a, d, A_KV_HEADS, A_GROUP, HEAD_DIM)
    wq = wq.transpose(0, 1, 3, 2, 4).reshape(n_a, d, A_Q_W) * QSCALE
    w_qkv = jnp.concatenate([wq, w_qkv_a[:, :, A_Q_W:]], axis=2).astype(BF16)
    w_oa = w_o_a.reshape(n_a, A_KV_HEADS, A_GROUP, HEAD_DIM, d).transpose(0, 2, 1, 3, 4)
    w_oa = w_oa.reshape(n_a, A_Q_W, d).astype(BF16)
    w_kv = w_kv_s.astype(BF16)[None]
    w_qb = (w_q_b * QSCALE).astype(BF16)
    w_ob = w_o_b.astype(BF16)
    w_gu = w_gate_up.astype(BF16)
    w_dn = w_down.astype(BF16)

    cak_t = cache_a_k.transpose(0, 1, 3, 4, 2)
    cav_t = cache_a_v.transpose(0, 1, 3, 4, 2)
    cbk_t = cache_b_k.transpose(0, 2, 3, 1)
    cbv_t = cache_b_v.transpose(0, 2, 3, 1)

    units_a = _units_a()
    a_k_p, a_v_p, a_k_s, a_v_s = [], [], [], []
    qkv_outs = ((0, A_Q_W, 1, BF16), (A_Q_W, A_KV_W, 1, F32), (A_Q_W + A_KV_W, A_KV_W, 1, F32))
    qkv_outs_s = ((0, A_Q_W, 1, F32),) + qkv_outs[1:]
    qb_outs = tuple((gi * B_W, B_W, r, BF16) for gi, (_, r) in enumerate(B_PATTERNS))
    qb_outs_s = tuple((gi * B_W, B_W, 1, F32) for gi in range(B_N_GROUPS))
    kv_outs = [(0, B_W, -wb_p, F32), (B_W, B_W, -wb_p, F32)]
    for _, r in B_PATTERNS:
        kv_outs += [(0, B_W, r, BF16), (B_W, B_W, r, BF16)]
    kv_outs_s = [(0, B_W, 1, F32), (B_W, B_W, 1, F32)]

    def flat(x):
        return x.reshape(-1, x.shape[-1])

    for l in range(depth):
        last = l == depth - 1
        gfin = g_final if last else None
        if l < n_a:
            qp, kp, vp = norm_mm(xp, g_attn[l], w_qkv, l, qkv_outs)
            qs, ks_, vs_ = norm_mm(xs, g_attn[l], w_qkv, l, qkv_outs_s)
            op = band_attn(qp[:, None], kp[:, None], vp[:, None], sinks_a[l], units_a, 1,
                           A_WINDOW - 1, False, unroll=8)
            osm, ko, vo = sample_attn_a(sinks_a[l], flat(qs), flat(ks_), flat(vs_), cak_t, cav_t, l, ts)
            a_k_p.append(kp[:, -wa_p:].reshape(bp, wa_p, A_KV_HEADS, HEAD_DIM))
            a_v_p.append(vp[:, -wa_p:].reshape(bp, wa_p, A_KV_HEADS, HEAD_DIM))
            a_k_s.append(ko.transpose(0, 3, 1, 2))
            a_v_s.append(vo.transpose(0, 3, 1, 2))
            xp = proj_ffn(flat(xp), flat(op), None, w_oa, l, g_ffn[l], w_gu, w_dn, l, gfin, tm=512)
            xs = proj_ffn(flat(xs), osm, None, w_oa, l, g_ffn[l], w_gu, w_dn, l, gfin)
            xp = xp.reshape(bp, tp, d)
            xs = xs.reshape(1, bs * ts, d)
        else:
            b = l - n_a
            if l == n_a:
                kv = norm_mm(xp, g_kv, w_kv, 0, kv_outs)
                kbp, vbp = kv[0], kv[1]
                k_sub, v_sub = kv[2::2], kv[3::2]
                kbs, vbs = norm_mm(xs, g_kv, w_kv, 0, kv_outs_s)
                kbs, vbs = flat(kbs), flat(vbs)
            q_groups = norm_mm(xp, g_attn[l], w_qb, b, qb_outs)
            outs, lses = [], []
            for gi, (w, r) in enumerate(B_PATTERNS):
                q_sub = q_groups[gi] if r > 1 else q_groups[gi][:, None]
                k_g = k_sub[gi] if r > 1 else k_sub[gi][:, None]
                v_g = v_sub[gi] if r > 1 else v_sub[gi][:, None]
                o_g, lse_g = band_attn(q_sub, k_g, v_g, None, _units_b(gi), r, w // r, True, unroll=8)
                outs.append(flat(o_g))
                lses.append(lse_g)
            qs_groups = [flat(q) for q in norm_mm(xs, g_attn[l], w_qb, b, qb_outs_s)]
            if l == n_a:
                osm, kcb_t, vcb_t = sample_attn_b(qs_groups, kbs, vbs, cbk_t, cbv_t, ts, True)
            else:
                osm = sample_attn_b(qs_groups, kbs, vbs, cbk_t, cbv_t, ts, False, bb=2)[0]
            xp = proj_ffn(flat(xp), outs, lses, w_ob, b, g_ffn[l], w_gu, w_dn, l, gfin, tm=512)
            xs = proj_ffn(flat(xs), osm, None, w_ob, b, g_ffn[l], w_gu, w_dn, l, gfin)
            xp = xp.reshape(bp, tp, d)
            xs = xs.reshape(1, bs * ts, d)

    y_sample = xs.reshape(bs, ts, d)
    b_k_p = kbp.reshape(bp, wb_p, B_SLOTS, HEAD_DIM)
    b_v_p = vbp.reshape(bp, wb_p, B_SLOTS, HEAD_DIM)
    return (xp, y_sample,
            jnp.stack(a_k_p, axis=0), jnp.stack(a_v_p, axis=0), b_k_p, b_v_p,
            jnp.stack(a_k_s, axis=0), jnp.stack(a_v_s, axis=0),
            kcb_t.transpose(0, 3, 1, 2), vcb_t.transpose(0, 3, 1, 2))
```

```python
import functools
import math

import jax
import jax.numpy as jnp
from jax import lax
from jax.experimental import pallas as pl
from jax.experimental.pallas import tpu as pltpu

D_MODEL = 1024
HEAD_DIM = 64
A_HEADS = 16
A_KV_HEADS = 4
A_GROUP = A_HEADS // A_KV_HEADS
A_WINDOW = 128
A_Q_W = A_HEADS * HEAD_DIM
A_KV_W = A_KV_HEADS * HEAD_DIM
B_SLOTS = 8
B_PATTERNS = ((128, 1), (512, 4), (2048, 16))
B_N_GROUPS = len(B_PATTERNS)
B_HEADS = B_N_GROUPS * B_SLOTS
B_W = B_SLOTS * HEAD_DIM
BLOCK = 128
EPS = 1e-6
NEG = -1e30
LOG2E = math.log2(math.e)
LN2 = math.log(2.0)
QSCALE = HEAD_DIM ** -0.5 * LOG2E

LANES = 128
HALF = LANES // 2
SUPER = 16 * BLOCK
MAX_ROW_STRIDE = 4
VMEM_LIMIT = 56 * 1024 * 1024

F32 = jnp.float32
BF16 = jnp.bfloat16
NT_DIMS = (((1,), (1,)), ((), ()))


def _slopes(n):
    return [float(2.0 ** (-8.0 * (i + 1) / n)) for i in range(n)]


SLOPES_A = _slopes(A_HEADS)
SLOPES_B = _slopes(B_HEADS)


def _rms(x, g):
    ms = jnp.mean(x * x, axis=-1, keepdims=True)
    return (x * lax.rsqrt(ms + EPS)) * g


def _params(n_grid):
    return pltpu.CompilerParams(dimension_semantics=("arbitrary",) * n_grid,
                                vmem_limit_bytes=VMEM_LIMIT)


def _const_spec(shape):
    nd = len(shape)
    return pl.BlockSpec(shape, lambda *_: (0,) * nd, pipeline_mode=pl.Buffered(1))


def _layer_spec(w, l):
    return pl.BlockSpec((None,) + w.shape[1:], lambda *_: (l, 0, 0), pipeline_mode=pl.Buffered(1))


def _norm_mm_kernel(x_ref, g_ref, w_ref, *refs, outs):
    out_refs, scr_refs = refs[:len(outs)], refs[len(outs):]
    h = _rms(x_ref[...], g_ref[...])
    y = jnp.dot(h.astype(BF16), w_ref[...], preferred_element_type=F32)
    si = 0
    for o_ref, (off, width, r, _) in zip(out_refs, outs):
        yo = y[:, off:off + width]
        if r <= 1:
            o_ref[...] = yo.astype(o_ref.dtype)
        else:
            scr, scr_b = scr_refs[si], scr_refs[si + 1]
            si += 2
            tm = yo.shape[0]
            for j in range(width // LANES):
                cols = slice(j * LANES, (j + 1) * LANES)
                scr[j] = yo[:, cols]
                if r <= MAX_ROW_STRIDE:
                    for c in range(r):
                        o_ref[c, :, cols] = scr[j, pl.ds(c, tm // r, stride=r), :].astype(o_ref.dtype)
                else:
                    r1, r2 = MAX_ROW_STRIDE, r // MAX_ROW_STRIDE
                    q = tm // r1
                    for c1 in range(r1):
                        scr_b[j, c1 * q:(c1 + 1) * q, :] = scr[j, pl.ds(c1, q, stride=r1), :]
                    for c1 in range(r1):
                        for c2 in range(r2):
                            o_ref[c2 * r1 + c1, :, cols] = scr_b[
                                j, pl.ds(c1 * q + c2, tm // r, stride=r2), :].astype(o_ref.dtype)


def norm_mm(x, g, w, l, outs, tm=512):
    bsz, t, d = x.shape
    assert t % tm == 0
    in_specs = [pl.BlockSpec((None, tm, d), lambda b, j: (b, j, 0)), _const_spec((1, d)),
                _layer_spec(w, l)]
    out_specs, out_shape, scratch = [], [], []
    for (_, width, r, dt) in outs:
        if r == 1:
            out_specs.append(pl.BlockSpec((None, tm, width), lambda b, j: (b, j, 0)))
            out_shape.append(jax.ShapeDtypeStruct((bsz, t, width), dt))
        elif r < 0:
            tail = -r
            assert tail % tm == 0 and tail <= t
            first = (t - tail) // tm
            out_specs.append(pl.BlockSpec((None, tm, width),
                                          lambda b, j, first=first: (b, jnp.maximum(j - first, 0), 0)))
            out_shape.append(jax.ShapeDtypeStruct((bsz, tail, width), dt))
        else:
            assert (tm // r) % 16 == 0
            out_specs.append(pl.BlockSpec((None, r, tm // r, width), lambda b, j: (b, 0, j, 0)))
            out_shape.append(jax.ShapeDtypeStruct((bsz, r, t // r, width), dt))
            assert r <= MAX_ROW_STRIDE or (r % MAX_ROW_STRIDE == 0 and r // MAX_ROW_STRIDE <= MAX_ROW_STRIDE)
            scratch += [pltpu.VMEM((width // LANES, tm, LANES), F32)] * 2
    return pl.pallas_call(
        functools.partial(_norm_mm_kernel, outs=tuple(outs)),
        grid=(bsz, t // tm),
        in_specs=in_specs,
        out_specs=out_specs,
        out_shape=out_shape,
        scratch_shapes=scratch,
        compiler_params=_params(2),
        name="norm_mm",
    )(x, g.reshape(1, d), w)


def _proj_ffn_kernel(*refs, combine, final, d_ff):
    it = iter(refs)
    x_ref = next(it)
    if combine:
        o_refs = [next(it) for _ in range(B_N_GROUPS)]
        l_refs = [next(it) for _ in range(B_N_GROUPS)]
    else:
        o_ref = next(it)
    wo_ref, g_ref, wgu_ref, wdn_ref = next(it), next(it), next(it), next(it)
    gfin_ref = next(it) if final else None
    out_ref = next(it)

    if combine:
        ls = [jnp.concatenate([r[j] for j in range(r.shape[0])], axis=1) for r in l_refs]
        mx = jnp.maximum(jnp.maximum(ls[0], ls[1]), ls[2])
        es = [jnp.exp2(l - mx) for l in ls]
        den = es[0] + es[1] + es[2]
        num = es[0] * o_refs[0][...].astype(F32)
        num = num + es[1] * o_refs[1][...].astype(F32)
        num = num + es[2] * o_refs[2][...].astype(F32)
        o = (num / den).astype(BF16)
    else:
        o = o_ref[...].astype(BF16)
    x = x_ref[...] + jnp.dot(o, wo_ref[...], preferred_element_type=F32)
    h = _rms(x, g_ref[...]).astype(BF16)
    gu = jnp.dot(h, wgu_ref[...], preferred_element_type=F32)
    gate = gu[:, :d_ff]
    up = gu[:, d_ff:]
    act = (gate / (1.0 + jnp.exp(-gate))) * up
    y = x + jnp.dot(act.astype(BF16), wdn_ref[...], preferred_element_type=F32)
    if final:
        y = _rms(y, gfin_ref[...])
    out_ref[...] = y


def proj_ffn(x, o, lses, w_o, lo, g, w_gu, w_dn, l, g_final=None, tm=256):
    rows, d = x.shape
    combine = lses is not None
    final = g_final is not None
    d_ff = w_dn.shape[1]
    ow = w_o.shape[1]
    row_spec = lambda w: pl.BlockSpec((tm, w), lambda i: (i, 0))
    args, in_specs = [x], [row_spec(d)]
    if combine:
        args += list(o) + list(lses)
        tpb = lses[0].shape[2] // tm
        in_specs += [row_spec(ow)] * B_N_GROUPS
        in_specs += [pl.BlockSpec((None, ow // LANES, tm, LANES),
                                  lambda i: (i // tpb, 0, i % tpb, 0))] * B_N_GROUPS
    else:
        args.append(o)
        in_specs.append(row_spec(ow))
    args += [w_o, g.reshape(1, d), w_gu, w_dn]
    in_specs += [_layer_spec(w_o, lo), _const_spec((1, d)), _layer_spec(w_gu, l), _layer_spec(w_dn, l)]
    if final:
        args.append(g_final.reshape(1, d))
        in_specs.append(_const_spec((1, d)))
    return pl.pallas_call(
        functools.partial(_proj_ffn_kernel, combine=combine, final=final, d_ff=d_ff),
        grid=(rows // tm,),
        in_specs=in_specs,
        out_specs=row_spec(d),
        out_shape=jax.ShapeDtypeStruct((rows, d), F32),
        compiler_params=_params(1),
        name="proj_ffn",
    )(*args)


def _band_attn_kernel(*refs, units, n_kv_pairs, has_sink, emit_lse, r, max_dist, unroll):
    it = iter(refs)
    sink_ref = next(it) if has_sink else None
    q_ref, kc_ref, kp_ref, vc_ref, vp_ref = (next(it) for _ in range(5))
    o_ref = next(it)
    lse_ref = next(it) if emit_lse else None
    bias_ref = next(it)
    o_scr = next(it) if r > 1 else None
    n_blocks = SUPER // BLOCK
    nbeta = n_blocks // r
    step = pl.program_id(1)

    @pl.when((pl.program_id(0) == 0) & (step == 0))
    def _():
        qi = lax.broadcasted_iota(jnp.int32, (2 * BLOCK, 2 * BLOCK), 0) & (BLOCK - 1)
        kj = lax.broadcasted_iota(jnp.int32, (2 * BLOCK, 2 * BLOCK), 1)
        row = lax.broadcasted_iota(jnp.int32, (2 * BLOCK, 2 * BLOCK), 0)
        dist = BLOCK + qi - kj
        valid = (dist >= 0) & (dist <= max_dist)
        distf = (r * dist).astype(F32)
        for u, (_, _, slope_lo, slope_hi, _, _) in enumerate(units):
            slope = jnp.where(row < BLOCK, slope_lo * LOG2E, slope_hi * LOG2E)
            b = -(slope * distf)
            bias_ref[0, u] = jnp.where(valid, b, NEG)
            bias_ref[1, u] = jnp.where(valid & (kj >= BLOCK), b, NEG)

    lane = lax.broadcasted_iota(jnp.int32, (BLOCK, LANES), 1)
    lo = lane < HALF
    row1 = lax.broadcasted_iota(jnp.int32, (2 * BLOCK, 1), 0)

    def block(blk, carry):
        if nbeta == 1:
            c, beta = blk, 0
        elif r == 1:
            c, beta = 0, blk
        else:
            c = lax.shift_right_logical(blk, nbeta.bit_length() - 1)
            beta = blk & (nbeta - 1)
        if nbeta == 1:
            row0 = prow0 = 0
        else:
            row0 = pl.multiple_of(beta * BLOCK, BLOCK)
            prow0 = pl.multiple_of(jnp.maximum(beta - 1, 0) * BLOCK, BLOCK)
        first_beta = beta == 0
        sel = jnp.where(first_beta & (step == 0), 1, 0)

        def prev_cur(cur_ref, prev_ref, sl):
            cur = cur_ref[c, pl.ds(row0, BLOCK), sl]
            if nbeta == 1:
                prev = prev_ref[c, :, sl]
            else:
                prev = jnp.where(first_beta, prev_ref[c, :, sl], cur_ref[c, pl.ds(prow0, BLOCK), sl])
            return jnp.concatenate([prev, cur], axis=0).astype(BF16)

        kcat, vcat = [], []
        for p in range(n_kv_pairs):
            sl = slice(p * LANES, (p + 1) * LANES)
            kcat.append(prev_cur(kc_ref, kp_ref, sl))
            vcat.append(prev_cur(vc_ref, vp_ref, sl))

        heads = [(u, h) for u in range(len(units)) for h in range(2)]

        def scores(u, h):
            qoff, p = units[u][0], units[u][1]
            q = q_ref[c, pl.ds(row0, BLOCK), qoff:qoff + LANES].astype(F32)
            qh = (jnp.where(lo, q, 0.0) if h == 0 else jnp.where(lo, 0.0, q)).astype(BF16)
            return lax.dot_general(qh, kcat[p], NT_DIMS, preferred_element_type=F32)

        s_next = scores(*heads[0])
        stats = {}
        for idx, (u, h) in enumerate(heads):
            s = s_next
            if idx + 1 < len(heads):
                s_next = scores(*heads[idx + 1])
            p = units[u][1]
            t = s + bias_ref[sel, u, h * BLOCK:(h + 1) * BLOCK, :]
            m = jnp.max(t, axis=1, keepdims=True)
            e = jnp.exp2(t - m)
            l = jnp.sum(e, axis=1, keepdims=True)
            pv = jnp.dot(e.astype(BF16), vcat[p], preferred_element_type=F32)
            stats[(u, h)] = (m, l, pv)

        for u, (qoff, _, _, _, sink_lo, sink_hi) in enumerate(units):
            cols = slice(qoff, qoff + LANES)
            (m0, l0, pv0), (m1, l1, pv1) = stats[(u, 0)], stats[(u, 1)]
            m_u = jnp.where(lo, m0, m1)
            l_u = jnp.where(lo, l0, l1)
            if has_sink:
                sink_u = jnp.where(lo, sink_ref[sink_lo] * LOG2E, sink_ref[sink_hi] * LOG2E)
                l_u = l_u + jnp.exp2(sink_u - m_u)
            o_sub = jnp.where(lo, pv0, pv1) * (1.0 / l_u)
            if emit_lse:
                lse_sub = m_u + jnp.log2(l_u)
            jq = qoff // LANES
            if r == 1:
                o_ref[pl.ds(row0, BLOCK), cols] = o_sub.astype(o_ref.dtype)
                if emit_lse:
                    lse_ref[jq, pl.ds(row0, BLOCK), :] = lse_sub
            else:
                rows = pl.ds(beta * (BLOCK * r) + c, BLOCK, stride=r)
                o_scr[jq, rows, :] = o_sub
                if emit_lse:
                    lse_ref[jq, rows, :] = lse_sub
        return carry

    lax.fori_loop(0, n_blocks, block, 0, unroll=unroll)
    if r > 1:
        for jq in range(o_scr.shape[0]):
            o_ref[:, jq * LANES:(jq + 1) * LANES] = o_scr[jq].astype(o_ref.dtype)


def band_attn(q, k, v, sinks, units, r, max_dist, emit_lse, unroll=1):
    bsz, _, tr, qw = q.shape
    t = tr * r
    kw = k.shape[3]
    sr = SUPER // r
    has_sink = sinks is not None
    cur = lambda w: pl.BlockSpec((None, r, sr, w), lambda b, i: (b, 0, i, 0))
    prev = lambda w: pl.BlockSpec((None, r, BLOCK, w),
                                  lambda b, i: (b, 0, jnp.maximum(i * (sr // BLOCK) - 1, 0), 0))
    nat = lambda w: pl.BlockSpec((None, SUPER, w), lambda b, i: (b, i, 0))
    args, in_specs = [], []
    if has_sink:
        args.append(sinks)
        in_specs.append(pl.BlockSpec(memory_space=pltpu.SMEM))
    args += [q, k, k, v, v]
    in_specs += [cur(qw), cur(kw), prev(kw), cur(kw), prev(kw)]
    out_shape = [jax.ShapeDtypeStruct((bsz, t, qw), BF16)]
    out_specs = [nat(qw)]
    if emit_lse:
        out_shape.append(jax.ShapeDtypeStruct((bsz, qw // LANES, t, LANES), F32))
        out_specs.append(pl.BlockSpec((None, qw // LANES, SUPER, LANES), lambda b, i: (b, 0, i, 0)))
    scratch = [pltpu.VMEM((2, len(units), 2 * BLOCK, 2 * BLOCK), F32)]
    if r > 1:
        scratch.append(pltpu.VMEM((qw // LANES, SUPER, LANES), F32))
    res = pl.pallas_call(
        functools.partial(_band_attn_kernel, units=tuple(units), n_kv_pairs=kw // LANES,
                          has_sink=has_sink, emit_lse=emit_lse, r=r, max_dist=max_dist,
                          unroll=unroll),
        grid=(bsz, t // SUPER),
        in_specs=in_specs,
        out_specs=out_specs,
        out_shape=out_shape,
        scratch_shapes=scratch,
        compiler_params=_params(2),
        name="band_attn",
    )(*args)
    return res if emit_lse else res[0]


def _units_a():
    units = []
    for g in range(A_GROUP):
        for p in range(A_KV_HEADS // 2):
            h_lo = (2 * p) * A_GROUP + g
            h_hi = (2 * p + 1) * A_GROUP + g
            units.append((g * A_KV_W + p * LANES, p, SLOPES_A[h_lo], SLOPES_A[h_hi], h_lo, h_hi))
    return units


def _units_b(gi):
    units = []
    for p in range(B_SLOTS // 2):
        units.append((p * LANES, p, SLOPES_B[gi * B_SLOTS + 2 * p], SLOPES_B[gi * B_SLOTS + 2 * p + 1],
                      0, 0))
    return units


def _lane_group_mask(rows, width, group):
    lane = lax.broadcasted_iota(jnp.int32, (rows, width), 1)
    return (lane >= group * HEAD_DIM) & (lane < (group + 1) * HEAD_DIM)


def _block_diag_rows(q, n_heads):
    s, width = q.shape
    return jnp.concatenate(
        [jnp.where(_lane_group_mask(s, width, h), q, 0.0) for h in range(n_heads)], axis=0)


def _diag_rows(o_bd, n_heads):
    s = o_bd.shape[0] // n_heads
    width = o_bd.shape[1]
    acc = None
    for h in range(n_heads):
        term = jnp.where(_lane_group_mask(s, width, h), o_bd[h * s:(h + 1) * s], 0.0)
        acc = term if acc is None else acc + term
    return acc


def _pad_new(new_rows):
    s, c = new_rows.shape
    return jnp.concatenate([jnp.zeros((LANES - s, c), F32), new_rows], axis=0)


def _shifted(cache_t, new_pad, s):
    c, w = cache_t.shape
    rolled = pltpu.roll(cache_t, w - s, 1)
    lane_new = lax.broadcasted_iota(jnp.int32, (c, LANES), 1) >= LANES - s
    last = jnp.where(lane_new, new_pad.T, rolled[:, w - LANES:])
    if w == LANES:
        return last
    return jnp.concatenate([rolled[:, :w - LANES], last], axis=1)


def _sample_a_kernel(*refs, bb, s, layer, n_aliased):
    sink_ref, q_ref, kn_ref, vn_ref, kt_ref, vt_ref = refs[:6]
    o_ref, ko_ref, vo_ref, bias_c_ref, bias_n_ref = refs[6 + n_aliased:]
    if not n_aliased:
        for ll in range(ko_ref.shape[0]):
            if ll != layer:
                ko_ref[ll] = jnp.zeros(ko_ref.shape[1:], F32)
                vo_ref[ll] = jnp.zeros(vo_ref.shape[1:], F32)
        ko_ref, vo_ref = ko_ref.at[layer], vo_ref.at[layer]
    n_rows = A_HEADS * s
    w = kt_ref.shape[-1]
    row1 = lax.broadcasted_iota(jnp.int32, (n_rows, 1), 0)

    @pl.when(pl.program_id(0) == 0)
    def _():
        row = lax.broadcasted_iota(jnp.int32, (n_rows, w), 0)
        tok = row & (s - 1)
        j = lax.broadcasted_iota(jnp.int32, (n_rows, w), 1)
        slope = jnp.zeros((n_rows, w), F32)
        for g in range(A_GROUP):
            for kh in range(A_KV_HEADS):
                slope = jnp.where(row >= (g * A_KV_HEADS + kh) * s,
                                  SLOPES_A[kh * A_GROUP + g] * LOG2E, slope)
        dist_c = w + tok - j
        bias_c_ref[...] = jnp.where((dist_c >= 0) & (dist_c < A_WINDOW),
                                    -(slope * dist_c.astype(F32)), NEG)
        dist_n = tok - (j - (LANES - s))
        bias_n_ref[...] = jnp.where((j >= LANES - s) & (dist_n >= 0) & (dist_n < A_WINDOW),
                                    -(slope * dist_n.astype(F32)), NEG)

    sink = jnp.zeros((n_rows, 1), F32)
    for g in range(A_GROUP):
        for kh in range(A_KV_HEADS):
            sink = jnp.where(row1 >= (g * A_KV_HEADS + kh) * s,
                             sink_ref[kh * A_GROUP + g] * LOG2E, sink)
    bias_c = bias_c_ref[...]
    bias_n = bias_n_ref[...]

    outs = []
    for b in range(bb):
        rows = slice(b * s, (b + 1) * s)
        q = q_ref[rows, :]
        q_bd = jnp.concatenate(
            [_block_diag_rows(q[:, g * A_KV_W:(g + 1) * A_KV_W], A_KV_HEADS) for g in range(A_GROUP)],
            axis=0).astype(BF16)
        kt = kt_ref[b].reshape(A_KV_W, w)
        vt = vt_ref[b].reshape(A_KV_W, w)
        kn = _pad_new(kn_ref[rows, :])
        vn = _pad_new(vn_ref[rows, :])
        tc = jnp.dot(q_bd, kt.astype(BF16), preferred_element_type=F32) + bias_c
        tn = lax.dot_general(q_bd, kn.astype(BF16), NT_DIMS, preferred_element_type=F32) + bias_n
        m = jnp.maximum(jnp.max(tc, axis=1, keepdims=True), jnp.max(tn, axis=1, keepdims=True))
        ec = jnp.exp2(tc - m)
        en = jnp.exp2(tn - m)
        l = (jnp.sum(ec, axis=1, keepdims=True) + jnp.sum(en, axis=1, keepdims=True)
             + jnp.exp2(sink - m))
        inv = 1.0 / l
        o_bd = (lax.dot_general((ec * inv).astype(BF16), vt.astype(BF16), NT_DIMS,
                                preferred_element_type=F32)
                + jnp.dot((en * inv).astype(BF16), vn.astype(BF16), preferred_element_type=F32))
        grp = A_KV_HEADS * s
        outs.append(jnp.concatenate(
            [_diag_rows(o_bd[g * grp:(g + 1) * grp], A_KV_HEADS) for g in range(A_GROUP)], axis=1))
        ko_ref[b] = _shifted(kt, kn, s).reshape(ko_ref.shape[1:])
        vo_ref[b] = _shifted(vt, vn, s).reshape(vo_ref.shape[1:])
    o_ref[...] = jnp.concatenate(outs, axis=0).astype(o_ref.dtype)


def sample_attn_a(sinks, q, kn, vn, kt, vt, l, s, prev=None, bb=8):
    nl, n, kh, d, w = kt.shape
    rows_spec = lambda width: pl.BlockSpec((bb * s, width), lambda i: (i, 0))
    cache = pl.BlockSpec((None, bb, kh, d, w), lambda i: (l, i, 0, 0, 0))
    args = [sinks, q, kn, vn, kt, vt]
    in_specs = [pl.BlockSpec(memory_space=pltpu.SMEM), rows_spec(A_Q_W), rows_spec(A_KV_W),
                rows_spec(A_KV_W), cache, cache]
    aliases = {}
    cache_out = pl.BlockSpec((nl, bb, kh, d, w), lambda i: (0, i, 0, 0, 0))
    if prev is not None:
        aliases = {len(args): 1, len(args) + 1: 2}
        args += list(prev)
        in_specs += [pl.BlockSpec(memory_space=pl.ANY)] * 2
        cache_out = cache
    return pl.pallas_call(
        functools.partial(_sample_a_kernel, bb=bb, s=s, layer=l, n_aliased=len(aliases)),
        grid=(n // bb,),
        in_specs=in_specs,
        out_specs=[rows_spec(A_Q_W), cache_out, cache_out],
        out_shape=[jax.ShapeDtypeStruct((n * s, A_Q_W), BF16),
                   jax.ShapeDtypeStruct((nl, n, kh, d, w), F32),
                   jax.ShapeDtypeStruct((nl, n, kh, d, w), F32)],
        scratch_shapes=[pltpu.VMEM((A_HEADS * s, w), F32), pltpu.VMEM((A_HEADS * s, LANES), F32)],
        input_output_aliases=aliases,
        compiler_params=_params(1),
        name="sample_attn_a",
    )(*args)


def _sample_b_kernel(*refs, s, bb, emit_cache):
    it = iter(refs)
    q_refs = [next(it) for _ in range(B_N_GROUPS)]
    kn_ref, vn_ref, kt_ref, vt_ref, o_ref = (next(it) for _ in range(5))
    ko_ref, vo_ref = (next(it), next(it)) if emit_cache else (None, None)
    bias_refs = [next(it) for _ in range(B_N_GROUPS)]
    bias_n_ref = next(it)
    w = kt_ref.shape[-1]
    n_rows = B_SLOTS * s
    starts = [max(w - (-(-win // LANES)) * LANES, 0) for win, _ in B_PATTERNS]

    def bias_table(gi, n, dist_of_lane):
        win, r = B_PATTERNS[gi]
        assert r & (r - 1) == 0
        row = lax.broadcasted_iota(jnp.int32, (n_rows, n), 0)
        lane = lax.broadcasted_iota(jnp.int32, (n_rows, n), 1)
        slope = jnp.zeros((n_rows, n), F32)
        for slot in range(B_SLOTS):
            slope = jnp.where(row >= slot * s, SLOPES_B[gi * B_SLOTS + slot] * LOG2E, slope)
        dist, ok = dist_of_lane(row & (s - 1), lane)
        valid = ok & (dist >= 0) & (dist <= win) & ((dist & (r - 1)) == 0)
        return jnp.where(valid, -(slope * dist.astype(F32)), NEG)

    @pl.when(pl.program_id(0) == 0)
    def _():
        for gi in range(B_N_GROUPS):
            bias_refs[gi][...] = bias_table(
                gi, w - starts[gi], lambda tok, lane: (w + tok - (lane + starts[gi]), lane >= 0))
            bias_n_ref[gi] = bias_table(
                gi, LANES, lambda tok, lane: (tok - (lane - (LANES - s)), lane >= LANES - s))

    for b in range(bb):
        _sample_b_one(b, slice(b * s, (b + 1) * s), q_refs, kn_ref, vn_ref, kt_ref, vt_ref, o_ref,
                      ko_ref, vo_ref, bias_refs, bias_n_ref, starts, s)


def _sample_b_one(b, rows, q_refs, kn_ref, vn_ref, kt_ref, vt_ref, o_ref, ko_ref, vo_ref,
                  bias_refs, bias_n_ref, starts, s):
    w = kt_ref.shape[-1]
    kt = kt_ref[b].reshape(B_W, w)
    vt = vt_ref[b].reshape(B_W, w)
    kn = _pad_new(kn_ref[rows, :])
    vn = _pad_new(vn_ref[rows, :])
    kt16 = kt.astype(BF16)
    kn16 = kn.astype(BF16)
    es, ens, ls, lses = [], [], [], []
    for gi in range(B_N_GROUPS):
        q_bd = _block_diag_rows(q_refs[gi][rows, :], B_SLOTS).astype(BF16)
        tc = jnp.dot(q_bd, kt16[:, starts[gi]:], preferred_element_type=F32) + bias_refs[gi][...]
        tn = lax.dot_general(q_bd, kn16, NT_DIMS, preferred_element_type=F32) + bias_n_ref[gi]
        m = jnp.maximum(jnp.max(tc, axis=1, keepdims=True), jnp.max(tn, axis=1, keepdims=True))
        ec = jnp.exp2(tc - m)
        en = jnp.exp2(tn - m)
        l = jnp.sum(ec, axis=1, keepdims=True) + jnp.sum(en, axis=1, keepdims=True)
        es.append(ec)
        ens.append(en)
        ls.append(l)
        lses.append(m + jnp.log2(l))
    mx = jnp.maximum(jnp.maximum(lses[0], lses[1]), lses[2])
    ws = [jnp.exp2(x - mx) for x in lses]
    den = ws[0] + ws[1] + ws[2]
    cs = [ws[gi] / (den * ls[gi]) for gi in range(B_N_GROUPS)]
    tiles = []
    for tix in range(w // LANES):
        acc = None
        for gi in range(B_N_GROUPS):
            off = tix * LANES - starts[gi]
            if off >= 0:
                term = cs[gi] * es[gi][:, off:off + LANES]
                acc = term if acc is None else acc + term
        tiles.append(acc)
    pc = jnp.concatenate(tiles, axis=1).astype(BF16)
    pn = (cs[0] * ens[0] + cs[1] * ens[1] + cs[2] * ens[2]).astype(BF16)
    o_bd = (lax.dot_general(pc, vt.astype(BF16), NT_DIMS, preferred_element_type=F32)
            + jnp.dot(pn, vn.astype(BF16), preferred_element_type=F32))
    o_ref[rows, :] = _diag_rows(o_bd, B_SLOTS)
    if ko_ref is not None:
        ko_ref[b] = _shifted(kt, kn, s).reshape(ko_ref.shape[1:])
        vo_ref[b] = _shifted(vt, vn, s).reshape(vo_ref.shape[1:])


def sample_attn_b(qs, kn, vn, kt, vt, s, emit_cache, bb=1):
    n, slots, d, w = kt.shape
    rows_spec = pl.BlockSpec((bb * s, B_W), lambda i: (i, 0))
    cache = pl.BlockSpec((bb, slots, d, w), lambda i: (i, 0, 0, 0))
    out_specs = [rows_spec]
    out_shape = [jax.ShapeDtypeStruct((n * s, B_W), F32)]
    if emit_cache:
        out_specs += [cache, cache]
        out_shape += [jax.ShapeDtypeStruct((n, slots, d, w), F32)] * 2
    starts = [max(w - (-(-win // LANES)) * LANES, 0) for win, _ in B_PATTERNS]
    scratch = [pltpu.VMEM((slots * s, w - st), F32) for st in starts]
    scratch.append(pltpu.VMEM((B_N_GROUPS, slots * s, LANES), F32))
    return pl.pallas_call(
        functools.partial(_sample_b_kernel, s=s, bb=bb, emit_cache=emit_cache),
        grid=(n // bb,),
        in_specs=[rows_spec] * (B_N_GROUPS + 2) + [cache, cache],
        out_specs=out_specs,
        out_shape=out_shape,
        scratch_shapes=scratch,
        compiler_params=_params(1),
        name="sample_attn_b",
    )(*qs, kn, vn, kt, vt)


def kernel(x_prompt, x_sample, cache_a_k, cache_a_v, cache_b_k, cache_b_v, g_attn, g_ffn, w_qkv_a,
           sinks_a, w_o_a, g_kv, w_kv_s, w_q_b, w_o_b, w_gate_up, w_down, g_final):
    bp, tp, d = x_prompt.shape
    bs, ts, _ = x_sample.shape
    n_a = w_qkv_a.shape[0]
    depth = g_attn.shape[0]
    wa = cache_a_k.shape[2]
    wb = cache_b_k.shape[1]
    wa_p = min(A_WINDOW, tp)
    wb_p = min(max(w for w, _ in B_PATTERNS), tp)
    assert wa == LANES and wb % LANES == 0 and tp % SUPER == 0

    xp = x_prompt
    xs = x_sample.reshape(1, bs * ts, d)

    wq = w_qkv_a[:, :, :A_Q_W].reshape(n_a, d, A_KV_HEADS, A_GROUP, HEAD_DIM)
    wq = wq.transpose(0, 1, 3, 2, 4).reshape(n_a, d, A_Q_W) * QSCALE
    w_qkv = jnp.concatenate([wq, w_qkv_a[:, :, A_Q_W:]], axis=2).astype(BF16)
    w_oa = w_o_a.reshape(n_a, A_KV_HEADS, A_GROUP, HEAD_DIM, d).transpose(0, 2, 1, 3, 4)
    w_oa = w_oa.reshape(n_a, A_Q_W, d).astype(BF16)
    w_kv = w_kv_s.astype(BF16)[None]
    w_qb = (w_q_b * QSCALE).astype(BF16)
    w_ob = w_o_b.astype(BF16)
    w_gu = w_gate_up.astype(BF16)
    w_dn = w_down.astype(BF16)

    cak_t = cache_a_k.transpose(0, 1, 3, 4, 2)
    cav_t = cache_a_v.transpose(0, 1, 3, 4, 2)
    cbk_t = cache_b_k.transpose(0, 2, 3, 1)
    cbv_t = cache_b_v.transpose(0, 2, 3, 1)

    units_a = _units_a()
    a_k_p, a_v_p = [], []
    a_cache = None
    qkv_outs = ((0, A_Q_W, 1, BF16), (A_Q_W, A_KV_W, 1, F32), (A_Q_W + A_KV_W, A_KV_W, 1, F32))
    qkv_outs_s = ((0, A_Q_W, 1, F32),) + qkv_outs[1:]
    qb_outs = tuple((gi * B_W, B_W, r, BF16) for gi, (_, r) in enumerate(B_PATTERNS))
    qb_outs_s = tuple((gi * B_W, B_W, 1, F32) for gi in range(B_N_GROUPS))
    kv_outs = [(0, B_W, -wb_p, F32), (B_W, B_W, -wb_p, F32)]
    for _, r in B_PATTERNS:
        kv_outs += [(0, B_W, r, BF16), (B_W, B_W, r, BF16)]
    kv_outs_s = [(0, B_W, 1, F32), (B_W, B_W, 1, F32)]

    def flat(x):
        return x.reshape(-1, x.shape[-1])

    for l in range(depth):
        last = l == depth - 1
        gfin = g_final if last else None
        if l < n_a:
            qp, kp, vp = norm_mm(xp, g_attn[l], w_qkv, l, qkv_outs)
            qs, ks_, vs_ = norm_mm(xs, g_attn[l], w_qkv, l, qkv_outs_s)
            op = band_attn(qp[:, None], kp[:, None], vp[:, None], sinks_a[l], units_a, 1,
                           A_WINDOW - 1, False, unroll=8)
            osm, ko, vo = sample_attn_a(sinks_a[l], flat(qs), flat(ks_), flat(vs_), cak_t, cav_t, l, ts,
                                        prev=a_cache)
            a_cache = (ko, vo)
            a_k_p.append(kp[:, -wa_p:].reshape(bp, wa_p, A_KV_HEADS, HEAD_DIM))
            a_v_p.append(vp[:, -wa_p:].reshape(bp, wa_p, A_KV_HEADS, HEAD_DIM))
            xp = proj_ffn(flat(xp), flat(op), None, w_oa, l, g_ffn[l], w_gu, w_dn, l, gfin, tm=512)
            xs = proj_ffn(flat(xs), osm, None, w_oa, l, g_ffn[l], w_gu, w_dn, l, gfin)
            xp = xp.reshape(bp, tp, d)
            xs = xs.reshape(1, bs * ts, d)
        else:
            b = l - n_a
            if l == n_a:
                kv = norm_mm(xp, g_kv, w_kv, 0, kv_outs)
                kbp, vbp = kv[0], kv[1]
                k_sub, v_sub = kv[2::2], kv[3::2]
                kbs, vbs = norm_mm(xs, g_kv, w_kv, 0, kv_outs_s)
                kbs, vbs = flat(kbs), flat(vbs)
            q_groups = norm_mm(xp, g_attn[l], w_qb, b, qb_outs)
            outs, lses = [], []
            for gi, (w, r) in enumerate(B_PATTERNS):
                q_sub = q_groups[gi] if r > 1 else q_groups[gi][:, None]
                k_g = k_sub[gi] if r > 1 else k_sub[gi][:, None]
                v_g = v_sub[gi] if r > 1 else v_sub[gi][:, None]
                o_g, lse_g = band_attn(q_sub, k_g, v_g, None, _units_b(gi), r, w // r, True, unroll=8)
                outs.append(flat(o_g))
                lses.append(lse_g)
            qs_groups = [flat(q) for q in norm_mm(xs, g_attn[l], w_qb, b, qb_outs_s)]
            if l == n_a:
                osm, kcb_t, vcb_t = sample_attn_b(qs_groups, kbs, vbs, cbk_t, cbv_t, ts, True)
            else:
                osm = sample_attn_b(qs_groups, kbs, vbs, cbk_t, cbv_t, ts, False, bb=2)[0]
            xp = proj_ffn(flat(xp), outs, lses, w_ob, b, g_ffn[l], w_gu, w_dn, l, gfin, tm=512)
            xs = proj_ffn(flat(xs), osm, None, w_ob, b, g_ffn[l], w_gu, w_dn, l, gfin)
            xp = xp.reshape(bp, tp, d)
            xs = xs.reshape(1, bs * ts, d)

    y_sample = xs.reshape(bs, ts, d)
    b_k_p = kbp.reshape(bp, wb_p, B_SLOTS, HEAD_DIM)
    b_v_p = vbp.reshape(bp, wb_p, B_SLOTS, HEAD_DIM)
    return (xp, y_sample,
            jnp.stack(a_k_p, axis=0), jnp.stack(a_v_p, axis=0), b_k_p, b_v_p,
            a_cache[0].transpose(0, 1, 4, 2, 3), a_cache[1].transpose(0, 1, 4, 2, 3),
            kcb_t.transpose(0, 3, 1, 2), vcb_t.transpose(0, 3, 1, 2))
```

```python
import functools
import math

import jax
import jax.numpy as jnp
from jax import lax
from jax.experimental import pallas as pl
from jax.experimental.pallas import tpu as pltpu

D_MODEL = 1024
HEAD_DIM = 64
A_HEADS = 16
A_KV_HEADS = 4
A_GROUP = A_HEADS // A_KV_HEADS
A_WINDOW = 128
A_Q_W = A_HEADS * HEAD_DIM
A_KV_W = A_KV_HEADS * HEAD_DIM
B_SLOTS = 8
B_PATTERNS = ((128, 1), (512, 4), (2048, 16))
B_N_GROUPS = len(B_PATTERNS)
B_HEADS = B_N_GROUPS * B_SLOTS
B_W = B_SLOTS * HEAD_DIM
BLOCK = 128
EPS = 1e-6
NEG = -1e30
LOG2E = math.log2(math.e)
LN2 = math.log(2.0)
QSCALE = HEAD_DIM ** -0.5 * LOG2E

LANES = 128
HALF = LANES // 2
SUPER = 16 * BLOCK
MAX_ROW_STRIDE = 4
VMEM_LIMIT = 56 * 1024 * 1024

F32 = jnp.float32
BF16 = jnp.bfloat16
NT_DIMS = (((1,), (1,)), ((), ()))


def _slopes(n):
    return [float(2.0 ** (-8.0 * (i + 1) / n)) for i in range(n)]


SLOPES_A = _slopes(A_HEADS)
SLOPES_B = _slopes(B_HEADS)


def _rms(x, g):
    ms = jnp.mean(x * x, axis=-1, keepdims=True)
    return (x * lax.rsqrt(ms + EPS)) * g


def _params(n_grid):
    return pltpu.CompilerParams(dimension_semantics=("arbitrary",) * n_grid,
                                vmem_limit_bytes=VMEM_LIMIT)


def _const_spec(shape):
    nd = len(shape)
    return pl.BlockSpec(shape, lambda *_: (0,) * nd, pipeline_mode=pl.Buffered(1))


def _layer_spec(w, l):
    return pl.BlockSpec((None,) + w.shape[1:], lambda *_: (l, 0, 0), pipeline_mode=pl.Buffered(1))


def _norm_mm_kernel(x_ref, g_ref, w_ref, *refs, outs):
    out_refs, scr_refs = refs[:len(outs)], refs[len(outs):]
    h = _rms(x_ref[...], g_ref[...])
    y = jnp.dot(h.astype(BF16), w_ref[...], preferred_element_type=F32)
    si = 0
    for o_ref, (off, width, r, _) in zip(out_refs, outs):
        yo = y[:, off:off + width]
        if r <= 1:
            o_ref[...] = yo.astype(o_ref.dtype)
        else:
            scr, scr_b = scr_refs[si], scr_refs[si + 1]
            si += 2
            tm = yo.shape[0]
            for j in range(width // LANES):
                cols = slice(j * LANES, (j + 1) * LANES)
                scr[j] = yo[:, cols]
                if r <= MAX_ROW_STRIDE:
                    for c in range(r):
                        o_ref[c, :, cols] = scr[j, pl.ds(c, tm // r, stride=r), :].astype(o_ref.dtype)
                else:
                    r1, r2 = MAX_ROW_STRIDE, r // MAX_ROW_STRIDE
                    q = tm // r1
                    for c1 in range(r1):
                        scr_b[j, c1 * q:(c1 + 1) * q, :] = scr[j, pl.ds(c1, q, stride=r1), :]
                    for c1 in range(r1):
                        for c2 in range(r2):
                            o_ref[c2 * r1 + c1, :, cols] = scr_b[
                                j, pl.ds(c1 * q + c2, tm // r, stride=r2), :].astype(o_ref.dtype)


def norm_mm(x, g, w, l, outs, tm=512):
    bsz, t, d = x.shape
    assert t % tm == 0
    in_specs = [pl.BlockSpec((None, tm, d), lambda b, j: (b, j, 0)), _const_spec((1, d)),
                _layer_spec(w, l)]
    out_specs, out_shape, scratch = [], [], []
    for (_, width, r, dt) in outs:
        if r == 1:
            out_specs.append(pl.BlockSpec((None, tm, width), lambda b, j: (b, j, 0)))
            out_shape.append(jax.ShapeDtypeStruct((bsz, t, width), dt))
        elif r < 0:
            tail = -r
            assert tail % tm == 0 and tail <= t
            first = (t - tail) // tm
            out_specs.append(pl.BlockSpec((None, tm, width),
                                          lambda b, j, first=first: (b, jnp.maximum(j - first, 0), 0)))
            out_shape.append(jax.ShapeDtypeStruct((bsz, tail, width), dt))
        else:
            assert (tm // r) % 16 == 0
            out_specs.append(pl.BlockSpec((None, r, tm // r, width), lambda b, j: (b, 0, j, 0)))
            out_shape.append(jax.ShapeDtypeStruct((bsz, r, t // r, width), dt))
            assert r <= MAX_ROW_STRIDE or (r % MAX_ROW_STRIDE == 0 and r // MAX_ROW_STRIDE <= MAX_ROW_STRIDE)
            scratch += [pltpu.VMEM((width // LANES, tm, LANES), F32)] * 2
    return pl.pallas_call(
        functools.partial(_norm_mm_kernel, outs=tuple(outs)),
        grid=(bsz, t // tm),
        in_specs=in_specs,
        out_specs=out_specs,
        out_shape=out_shape,
        scratch_shapes=scratch,
        compiler_params=_params(2),
        name="norm_mm",
    )(x, g.reshape(1, d), w)


def _proj_ffn_kernel(*refs, combine, final, d_ff):
    it = iter(refs)
    x_ref = next(it)
    if combine:
        o_refs = [next(it) for _ in range(B_N_GROUPS)]
        l_refs = [next(it) for _ in range(B_N_GROUPS)]
    else:
        o_ref = next(it)
    wo_ref, g_ref, wgu_ref, wdn_ref = next(it), next(it), next(it), next(it)
    gfin_ref = next(it) if final else None
    out_ref = next(it)

    if combine:
        ls = [jnp.concatenate([r[j] for j in range(r.shape[0])], axis=1) for r in l_refs]
        mx = jnp.maximum(jnp.maximum(ls[0], ls[1]), ls[2])
        es = [jnp.exp2(l - mx) for l in ls]
        den = es[0] + es[1] + es[2]
        num = es[0] * o_refs[0][...].astype(F32)
        num = num + es[1] * o_refs[1][...].astype(F32)
        num = num + es[2] * o_refs[2][...].astype(F32)
        o = (num / den).astype(BF16)
    else:
        o = o_ref[...].astype(BF16)
    x = x_ref[...] + jnp.dot(o, wo_ref[...], preferred_element_type=F32)
    h = _rms(x, g_ref[...]).astype(BF16)
    gu = jnp.dot(h, wgu_ref[...], preferred_element_type=F32)
    gate = gu[:, :d_ff]
    up = gu[:, d_ff:]
    act = (gate / (1.0 + jnp.exp(-gate))) * up
    y = x + jnp.dot(act.astype(BF16), wdn_ref[...], preferred_element_type=F32)
    if final:
        y = _rms(y, gfin_ref[...])
    out_ref[...] = y


def proj_ffn(x, o, lses, w_o, lo, g, w_gu, w_dn, l, g_final=None, tm=256):
    rows, d = x.shape
    combine = lses is not None
    final = g_final is not None
    d_ff = w_dn.shape[1]
    ow = w_o.shape[1]
    row_spec = lambda w: pl.BlockSpec((tm, w), lambda i: (i, 0))
    args, in_specs = [x], [row_spec(d)]
    if combine:
        args += list(o) + list(lses)
        tpb = lses[0].shape[2] // tm
        in_specs += [row_spec(ow)] * B_N_GROUPS
        in_specs += [pl.BlockSpec((None, ow // LANES, tm, LANES),
                                  lambda i: (i // tpb, 0, i % tpb, 0))] * B_N_GROUPS
    else:
        args.append(o)
        in_specs.append(row_spec(ow))
    args += [w_o, g.reshape(1, d), w_gu, w_dn]
    in_specs += [_layer_spec(w_o, lo), _const_spec((1, d)), _layer_spec(w_gu, l), _layer_spec(w_dn, l)]
    if final:
        args.append(g_final.reshape(1, d))
        in_specs.append(_const_spec((1, d)))
    return pl.pallas_call(
        functools.partial(_proj_ffn_kernel, combine=combine, final=final, d_ff=d_ff),
        grid=(rows // tm,),
        in_specs=in_specs,
        out_specs=row_spec(d),
        out_shape=jax.ShapeDtypeStruct((rows, d), F32),
        compiler_params=_params(1),
        name="proj_ffn",
    )(*args)


def _band_attn_kernel(*refs, units, n_kv_pairs, has_sink, emit_lse, r, max_dist, unroll):
    it = iter(refs)
    sink_ref = next(it) if has_sink else None
    q_ref, kc_ref, kp_ref, vc_ref, vp_ref = (next(it) for _ in range(5))
    o_ref = next(it)
    lse_ref = next(it) if emit_lse else None
    bias_ref = next(it)
    o_scr = next(it) if r > 1 else None
    two_pass = r > MAX_ROW_STRIDE
    r1, r2 = MAX_ROW_STRIDE, r // MAX_ROW_STRIDE
    o_mid = next(it) if two_pass else None
    lse_mid = next(it) if two_pass and emit_lse else None
    assert not two_pass or (SUPER // BLOCK == r and r2 <= MAX_ROW_STRIDE)
    n_blocks = SUPER // BLOCK
    nbeta = n_blocks // r
    step = pl.program_id(1)

    @pl.when((pl.program_id(0) == 0) & (step == 0))
    def _():
        qi = lax.broadcasted_iota(jnp.int32, (2 * BLOCK, 2 * BLOCK), 0) & (BLOCK - 1)
        kj = lax.broadcasted_iota(jnp.int32, (2 * BLOCK, 2 * BLOCK), 1)
        row = lax.broadcasted_iota(jnp.int32, (2 * BLOCK, 2 * BLOCK), 0)
        dist = BLOCK + qi - kj
        valid = (dist >= 0) & (dist <= max_dist)
        distf = (r * dist).astype(F32)
        for u, (_, _, slope_lo, slope_hi, _, _) in enumerate(units):
            slope = jnp.where(row < BLOCK, slope_lo * LOG2E, slope_hi * LOG2E)
            b = -(slope * distf)
            bias_ref[0, u] = jnp.where(valid, b, NEG)
            bias_ref[1, u] = jnp.where(valid & (kj >= BLOCK), b, NEG)

    lane = lax.broadcasted_iota(jnp.int32, (BLOCK, LANES), 1)
    lo = lane < HALF
    row1 = lax.broadcasted_iota(jnp.int32, (2 * BLOCK, 1), 0)

    def block(blk, carry):
        if nbeta == 1:
            c, beta = blk, 0
        elif r == 1:
            c, beta = 0, blk
        else:
            c = lax.shift_right_logical(blk, nbeta.bit_length() - 1)
            beta = blk & (nbeta - 1)
        if nbeta == 1:
            row0 = prow0 = 0
        else:
            row0 = pl.multiple_of(beta * BLOCK, BLOCK)
            prow0 = pl.multiple_of(jnp.maximum(beta - 1, 0) * BLOCK, BLOCK)
        first_beta = beta == 0
        sel = jnp.where(first_beta & (step == 0), 1, 0)

        def prev_cur(cur_ref, prev_ref, sl):
            cur = cur_ref[c, pl.ds(row0, BLOCK), sl]
            if nbeta == 1:
                prev = prev_ref[c, :, sl]
            else:
                prev = jnp.where(first_beta, prev_ref[c, :, sl], cur_ref[c, pl.ds(prow0, BLOCK), sl])
            return jnp.concatenate([prev, cur], axis=0).astype(BF16)

        kcat, vcat = [], []
        for p in range(n_kv_pairs):
            sl = slice(p * LANES, (p + 1) * LANES)
            kcat.append(prev_cur(kc_ref, kp_ref, sl))
            vcat.append(prev_cur(vc_ref, vp_ref, sl))

        heads = [(u, h) for u in range(len(units)) for h in range(2)]

        def scores(u, h):
            qoff, p = units[u][0], units[u][1]
            q = q_ref[c, pl.ds(row0, BLOCK), qoff:qoff + LANES].astype(F32)
            qh = (jnp.where(lo, q, 0.0) if h == 0 else jnp.where(lo, 0.0, q)).astype(BF16)
            return lax.dot_general(qh, kcat[p], NT_DIMS, preferred_element_type=F32)

        s_next = scores(*heads[0])
        stats = {}
        for idx, (u, h) in enumerate(heads):
            s = s_next
            if idx + 1 < len(heads):
                s_next = scores(*heads[idx + 1])
            p = units[u][1]
            t = s + bias_ref[sel, u, h * BLOCK:(h + 1) * BLOCK, :]
            m = jnp.max(t, axis=1, keepdims=True)
            e = jnp.exp2(t - m)
            l = jnp.sum(e, axis=1, keepdims=True)
            pv = jnp.dot(e.astype(BF16), vcat[p], preferred_element_type=F32)
            stats[(u, h)] = (m, l, pv)

        for u, (qoff, _, _, _, sink_lo, sink_hi) in enumerate(units):
            cols = slice(qoff, qoff + LANES)
            (m0, l0, pv0), (m1, l1, pv1) = stats[(u, 0)], stats[(u, 1)]
            m_u = jnp.where(lo, m0, m1)
            l_u = jnp.where(lo, l0, l1)
            if has_sink:
                sink_u = jnp.where(lo, sink_ref[sink_lo] * LOG2E, sink_ref[sink_hi] * LOG2E)
                l_u = l_u + jnp.exp2(sink_u - m_u)
            o_sub = jnp.where(lo, pv0, pv1) * (1.0 / l_u)
            if emit_lse:
                lse_sub = m_u + jnp.log2(l_u)
            jq = qoff // LANES
            if r == 1:
                o_ref[pl.ds(row0, BLOCK), cols] = o_sub.astype(o_ref.dtype)
                if emit_lse:
                    lse_ref[jq, pl.ds(row0, BLOCK), :] = lse_sub
            elif not two_pass:
                rows = pl.ds(beta * (BLOCK * r) + c, BLOCK, stride=r)
                o_scr[jq, rows, :] = o_sub
                if emit_lse:
                    lse_ref[jq, rows, :] = lse_sub
            else:
                rows = pl.ds((c & (r1 - 1)) * (SUPER // r1) + lax.shift_right_logical(c, r1.bit_length() - 1),
                             BLOCK, stride=r2)
                o_mid[jq, rows, :] = o_sub
                if emit_lse:
                    lse_mid[jq, rows, :] = lse_sub
        return carry

    lax.fori_loop(0, n_blocks, block, 0, unroll=unroll)
    if r > 1:
        for jq in range(o_scr.shape[0]):
            if two_pass:
                for c1 in range(r1):
                    chunk = slice(c1 * (SUPER // r1), (c1 + 1) * (SUPER // r1))
                    o_scr[jq, pl.ds(c1, SUPER // r1, stride=r1), :] = o_mid[jq, chunk, :]
                    if emit_lse:
                        lse_ref[jq, pl.ds(c1, SUPER // r1, stride=r1), :] = lse_mid[jq, chunk, :]
            o_ref[:, jq * LANES:(jq + 1) * LANES] = o_scr[jq].astype(o_ref.dtype)


def band_attn(q, k, v, sinks, units, r, max_dist, emit_lse, unroll=1):
    bsz, _, tr, qw = q.shape
    t = tr * r
    kw = k.shape[3]
    sr = SUPER // r
    has_sink = sinks is not None
    cur = lambda w: pl.BlockSpec((None, r, sr, w), lambda b, i: (b, 0, i, 0))
    prev = lambda w: pl.BlockSpec((None, r, BLOCK, w),
                                  lambda b, i: (b, 0, jnp.maximum(i * (sr // BLOCK) - 1, 0), 0))
    nat = lambda w: pl.BlockSpec((None, SUPER, w), lambda b, i: (b, i, 0))
    args, in_specs = [], []
    if has_sink:
        args.append(sinks)
        in_specs.append(pl.BlockSpec(memory_space=pltpu.SMEM))
    args += [q, k, k, v, v]
    in_specs += [cur(qw), cur(kw), prev(kw), cur(kw), prev(kw)]
    out_shape = [jax.ShapeDtypeStruct((bsz, t, qw), BF16)]
    out_specs = [nat(qw)]
    if emit_lse:
        out_shape.append(jax.ShapeDtypeStruct((bsz, qw // LANES, t, LANES), F32))
        out_specs.append(pl.BlockSpec((None, qw // LANES, SUPER, LANES), lambda b, i: (b, 0, i, 0)))
    scratch = [pltpu.VMEM((2, len(units), 2 * BLOCK, 2 * BLOCK), F32)]
    if r > 1:
        scratch.append(pltpu.VMEM((qw // LANES, SUPER, LANES), F32))
    if r > MAX_ROW_STRIDE:
        scratch += [pltpu.VMEM((qw // LANES, SUPER, LANES), F32)] * (2 if emit_lse else 1)
    res = pl.pallas_call(
        functools.partial(_band_attn_kernel, units=tuple(units), n_kv_pairs=kw // LANES,
                          has_sink=has_sink, emit_lse=emit_lse, r=r, max_dist=max_dist,
                          unroll=unroll),
        grid=(bsz, t // SUPER),
        in_specs=in_specs,
        out_specs=out_specs,
        out_shape=out_shape,
        scratch_shapes=scratch,
        compiler_params=_params(2),
        name="band_attn",
    )(*args)
    return res if emit_lse else res[0]


def _units_a():
    units = []
    for g in range(A_GROUP):
        for p in range(A_KV_HEADS // 2):
            h_lo = (2 * p) * A_GROUP + g
            h_hi = (2 * p + 1) * A_GROUP + g
            units.append((g * A_KV_W + p * LANES, p, SLOPES_A[h_lo], SLOPES_A[h_hi], h_lo, h_hi))
    return units


def _units_b(gi):
    units = []
    for p in range(B_SLOTS // 2):
        units.append((p * LANES, p, SLOPES_B[gi * B_SLOTS + 2 * p], SLOPES_B[gi * B_SLOTS + 2 * p + 1],
                      0, 0))
    return units


def _lane_group_mask(rows, width, group):
    lane = lax.broadcasted_iota(jnp.int32, (rows, width), 1)
    return (lane >= group * HEAD_DIM) & (lane < (group + 1) * HEAD_DIM)


def _block_diag_rows(q, n_heads):
    s, width = q.shape
    return jnp.concatenate(
        [jnp.where(_lane_group_mask(s, width, h), q, 0.0) for h in range(n_heads)], axis=0)


def _diag_rows(o_bd, n_heads):
    s = o_bd.shape[0] // n_heads
    width = o_bd.shape[1]
    acc = None
    for h in range(n_heads):
        term = jnp.where(_lane_group_mask(s, width, h), o_bd[h * s:(h + 1) * s], 0.0)
        acc = term if acc is None else acc + term
    return acc


def _pad_new(new_rows):
    s, c = new_rows.shape
    return jnp.concatenate([jnp.zeros((LANES - s, c), F32), new_rows], axis=0)


def _shifted(cache_t, new_t, s):
    c, w = cache_t.shape
    rolled = pltpu.roll(cache_t, w - s, 1)
    lane_new = lax.broadcasted_iota(jnp.int32, (c, LANES), 1) >= LANES - s
    last = jnp.where(lane_new, new_t, rolled[:, w - LANES:])
    if w == LANES:
        return last
    return jnp.concatenate([rolled[:, :w - LANES], last], axis=1)


def _sample_a_kernel(*refs, bb, s, layer, n_aliased):
    sink_ref, q_ref, kn_ref, vn_ref, kt_ref, vt_ref = refs[:6]
    o_ref, ko_ref, vo_ref, bias_c_ref, bias_n_ref = refs[6 + n_aliased:]
    if not n_aliased:
        for ll in range(ko_ref.shape[0]):
            if ll != layer:
                ko_ref[ll] = jnp.zeros(ko_ref.shape[1:], F32)
                vo_ref[ll] = jnp.zeros(vo_ref.shape[1:], F32)
        ko_ref, vo_ref = ko_ref.at[layer], vo_ref.at[layer]
    n_rows = A_HEADS * s
    w = kt_ref.shape[-1]
    row1 = lax.broadcasted_iota(jnp.int32, (n_rows, 1), 0)

    @pl.when(pl.program_id(0) == 0)
    def _():
        row = lax.broadcasted_iota(jnp.int32, (n_rows, w), 0)
        tok = row & (s - 1)
        j = lax.broadcasted_iota(jnp.int32, (n_rows, w), 1)
        slope = jnp.zeros((n_rows, w), F32)
        for g in range(A_GROUP):
            for kh in range(A_KV_HEADS):
                slope = jnp.where(row >= (g * A_KV_HEADS + kh) * s,
                                  SLOPES_A[kh * A_GROUP + g] * LOG2E, slope)
        dist_c = w + tok - j
        bias_c_ref[...] = jnp.where((dist_c >= 0) & (dist_c < A_WINDOW),
                                    -(slope * dist_c.astype(F32)), NEG)
        dist_n = tok - (j - (LANES - s))
        bias_n_ref[...] = jnp.where((j >= LANES - s) & (dist_n >= 0) & (dist_n < A_WINDOW),
                                    -(slope * dist_n.astype(F32)), NEG)

    sink = jnp.zeros((n_rows, 1), F32)
    for g in range(A_GROUP):
        for kh in range(A_KV_HEADS):
            sink = jnp.where(row1 >= (g * A_KV_HEADS + kh) * s,
                             sink_ref[kh * A_GROUP + g] * LOG2E, sink)
    bias_c = bias_c_ref[...]
    bias_n = bias_n_ref[...]

    outs = []
    for b in range(bb):
        rows = slice(b * s, (b + 1) * s)
        q = q_ref[rows, :]
        q_bd = jnp.concatenate(
            [_block_diag_rows(q[:, g * A_KV_W:(g + 1) * A_KV_W], A_KV_HEADS) for g in range(A_GROUP)],
            axis=0).astype(BF16)
        kt = kt_ref[b].reshape(A_KV_W, w)
        vt = vt_ref[b].reshape(A_KV_W, w)
        kn = _pad_new(kn_ref[rows, :])
        vn = _pad_new(vn_ref[rows, :])
        tc = jnp.dot(q_bd, kt.astype(BF16), preferred_element_type=F32) + bias_c
        tn = lax.dot_general(q_bd, kn.astype(BF16), NT_DIMS, preferred_element_type=F32) + bias_n
        m = jnp.maximum(jnp.max(tc, axis=1, keepdims=True), jnp.max(tn, axis=1, keepdims=True))
        ec = jnp.exp2(tc - m)
        en = jnp.exp2(tn - m)
        l = (jnp.sum(ec, axis=1, keepdims=True) + jnp.sum(en, axis=1, keepdims=True)
             + jnp.exp2(sink - m))
        inv = 1.0 / l
        o_bd = (lax.dot_general((ec * inv).astype(BF16), vt.astype(BF16), NT_DIMS,
                                preferred_element_type=F32)
                + jnp.dot((en * inv).astype(BF16), vn.astype(BF16), preferred_element_type=F32))
        grp = A_KV_HEADS * s
        outs.append(jnp.concatenate(
            [_diag_rows(o_bd[g * grp:(g + 1) * grp], A_KV_HEADS) for g in range(A_GROUP)], axis=1))
        ko_ref[b] = _shifted(kt, kn.T, s).reshape(ko_ref.shape[1:])
        vo_ref[b] = _shifted(vt, vn.T, s).reshape(vo_ref.shape[1:])
    o_ref[...] = jnp.concatenate(outs, axis=0).astype(o_ref.dtype)


def sample_attn_a(sinks, q, kn, vn, kt, vt, l, s, prev=None, bb=8):
    nl, n, kh, d, w = kt.shape
    rows_spec = lambda width: pl.BlockSpec((bb * s, width), lambda i: (i, 0))
    cache = pl.BlockSpec((None, bb, kh, d, w), lambda i: (l, i, 0, 0, 0))
    args = [sinks, q, kn, vn, kt, vt]
    in_specs = [pl.BlockSpec(memory_space=pltpu.SMEM), rows_spec(A_Q_W), rows_spec(A_KV_W),
                rows_spec(A_KV_W), cache, cache]
    aliases = {}
    cache_out = pl.BlockSpec((nl, bb, kh, d, w), lambda i: (0, i, 0, 0, 0))
    if prev is not None:
        aliases = {len(args): 1, len(args) + 1: 2}
        args += list(prev)
        in_specs += [pl.BlockSpec(memory_space=pl.ANY)] * 2
        cache_out = cache
    return pl.pallas_call(
        functools.partial(_sample_a_kernel, bb=bb, s=s, layer=l, n_aliased=len(aliases)),
        grid=(n // bb,),
        in_specs=in_specs,
        out_specs=[rows_spec(A_Q_W), cache_out, cache_out],
        out_shape=[jax.ShapeDtypeStruct((n * s, A_Q_W), BF16),
                   jax.ShapeDtypeStruct((nl, n, kh, d, w), F32),
                   jax.ShapeDtypeStruct((nl, n, kh, d, w), F32)],
        scratch_shapes=[pltpu.VMEM((A_HEADS * s, w), F32), pltpu.VMEM((A_HEADS * s, LANES), F32)],
        input_output_aliases=aliases,
        compiler_params=_params(1),
        name="sample_attn_a",
    )(*args)


def _sample_b_kernel(*refs, s, bb, emit_cache):
    it = iter(refs)
    q_refs = [next(it) for _ in range(B_N_GROUPS)]
    kn_ref, vn_ref, kt_ref, vt_ref, o_ref = (next(it) for _ in range(5))
    ko_ref, vo_ref = (next(it), next(it)) if emit_cache else (None, None)
    bias_refs = [next(it) for _ in range(B_N_GROUPS)]
    bias_n_ref = next(it)
    w = kt_ref.shape[-1]
    n_rows = B_SLOTS * s
    starts = [max(w - (-(-win // LANES)) * LANES, 0) for win, _ in B_PATTERNS]

    def bias_table(gi, n, dist_of_lane):
        win, r = B_PATTERNS[gi]
        assert r & (r - 1) == 0
        row = lax.broadcasted_iota(jnp.int32, (n_rows, n), 0)
        lane = lax.broadcasted_iota(jnp.int32, (n_rows, n), 1)
        slope = jnp.zeros((n_rows, n), F32)
        for slot in range(B_SLOTS):
            slope = jnp.where(row >= slot * s, SLOPES_B[gi * B_SLOTS + slot] * LOG2E, slope)
        dist, ok = dist_of_lane(row & (s - 1), lane)
        valid = ok & (dist >= 0) & (dist <= win) & ((dist & (r - 1)) == 0)
        return jnp.where(valid, -(slope * dist.astype(F32)), NEG)

    @pl.when(pl.program_id(0) == 0)
    def _():
        for gi in range(B_N_GROUPS):
            bias_refs[gi][...] = bias_table(
                gi, w - starts[gi], lambda tok, lane: (w + tok - (lane + starts[gi]), lane >= 0))
            bias_n_ref[gi] = bias_table(
                gi, LANES, lambda tok, lane: (tok - (lane - (LANES - s)), lane >= LANES - s))

    for b in range(bb):
        _sample_b_one(b, slice(b * s, (b + 1) * s), q_refs, kn_ref, vn_ref, kt_ref, vt_ref, o_ref,
                      ko_ref, vo_ref, bias_refs, bias_n_ref, starts, s)


def _sample_b_one(b, rows, q_refs, kn_ref, vn_ref, kt_ref, vt_ref, o_ref, ko_ref, vo_ref,
                  bias_refs, bias_n_ref, starts, s):
    w = kt_ref.shape[-1]
    kt = kt_ref[b].reshape(B_W, w)
    vt = vt_ref[b].reshape(B_W, w)
    kn = _pad_new(kn_ref[rows, :])
    vn = _pad_new(vn_ref[rows, :])
    kt16 = kt.astype(BF16)
    kn16 = kn.astype(BF16)
    es, ens, ls, lses = [], [], [], []
    for gi in range(B_N_GROUPS):
        q_bd = _block_diag_rows(q_refs[gi][rows, :], B_SLOTS).astype(BF16)
        tc = jnp.dot(q_bd, kt16[:, starts[gi]:], preferred_element_type=F32) + bias_refs[gi][...]
        tn = lax.dot_general(q_bd, kn16, NT_DIMS, preferred_element_type=F32) + bias_n_ref[gi]
        m = jnp.maximum(jnp.max(tc, axis=1, keepdims=True), jnp.max(tn, axis=1, keepdims=True))
        ec = jnp.exp2(tc - m)
        en = jnp.exp2(tn - m)
        l = jnp.sum(ec, axis=1, keepdims=True) + jnp.sum(en, axis=1, keepdims=True)
        es.append(ec)
        ens.append(en)
        ls.append(l)
        lses.append(m + jnp.log2(l))
    mx = jnp.maximum(jnp.maximum(lses[0], lses[1]), lses[2])
    ws = [jnp.exp2(x - mx) for x in lses]
    den = ws[0] + ws[1] + ws[2]
    cs = [ws[gi] / (den * ls[gi]) for gi in range(B_N_GROUPS)]
    tiles = []
    for tix in range(w // LANES):
        acc = None
        for gi in range(B_N_GROUPS):
            off = tix * LANES - starts[gi]
            if off >= 0:
                term = cs[gi] * es[gi][:, off:off + LANES]
                acc = term if acc is None else acc + term
        tiles.append(acc)
    pc = jnp.concatenate(tiles, axis=1).astype(BF16)
    pn = (cs[0] * ens[0] + cs[1] * ens[1] + cs[2] * ens[2]).astype(BF16)
    o_bd = (lax.dot_general(pc, vt.astype(BF16), NT_DIMS, preferred_element_type=F32)
            + jnp.dot(pn, vn.astype(BF16), preferred_element_type=F32))
    o_ref[rows, :] = _diag_rows(o_bd, B_SLOTS)
    if ko_ref is not None:
        ko_ref[b] = _shifted(kt, kn.T, s).reshape(ko_ref.shape[1:])
        vo_ref[b] = _shifted(vt, vn.T, s).reshape(vo_ref.shape[1:])


def sample_attn_b(qs, kn, vn, kt, vt, s, emit_cache, bb=1):
    n, slots, d, w = kt.shape
    rows_spec = pl.BlockSpec((bb * s, B_W), lambda i: (i, 0))
    cache = pl.BlockSpec((bb, slots, d, w), lambda i: (i, 0, 0, 0))
    out_specs = [rows_spec]
    out_shape = [jax.ShapeDtypeStruct((n * s, B_W), F32)]
    if emit_cache:
        out_specs += [cache, cache]
        out_shape += [jax.ShapeDtypeStruct((n, slots, d, w), F32)] * 2
    starts = [max(w - (-(-win // LANES)) * LANES, 0) for win, _ in B_PATTERNS]
    scratch = [pltpu.VMEM((slots * s, w - st), F32) for st in starts]
    scratch.append(pltpu.VMEM((B_N_GROUPS, slots * s, LANES), F32))
    return pl.pallas_call(
        functools.partial(_sample_b_kernel, s=s, bb=bb, emit_cache=emit_cache),
        grid=(n // bb,),
        in_specs=[rows_spec] * (B_N_GROUPS + 2) + [cache, cache],
        out_specs=out_specs,
        out_shape=out_shape,
        scratch_shapes=scratch,
        compiler_params=_params(1),
        name="sample_attn_b",
    )(*qs, kn, vn, kt, vt)


def kernel(x_prompt, x_sample, cache_a_k, cache_a_v, cache_b_k, cache_b_v, g_attn, g_ffn, w_qkv_a,
           sinks_a, w_o_a, g_kv, w_kv_s, w_q_b, w_o_b, w_gate_up, w_down, g_final):
    bp, tp, d = x_prompt.shape
    bs, ts, _ = x_sample.shape
    n_a = w_qkv_a.shape[0]
    depth = g_attn.shape[0]
    wa = cache_a_k.shape[2]
    wb = cache_b_k.shape[1]
    wa_p = min(A_WINDOW, tp)
    wb_p = min(max(w for w, _ in B_PATTERNS), tp)
    assert wa == LANES and wb % LANES == 0 and tp % SUPER == 0

    xp = x_prompt
    xs = x_sample.reshape(1, bs * ts, d)

    wq = w_qkv_a[:, :, :A_Q_W].reshape(n_a, d, A_KV_HEADS, A_GROUP, HEAD_DIM)
    wq = wq.transpose(0, 1, 3, 2, 4).reshape(n_a, d, A_Q_W) * QSCALE
    w_qkv = jnp.concatenate([wq, w_qkv_a[:, :, A_Q_W:]], axis=2).astype(BF16)
    w_oa = w_o_a.reshape(n_a, A_KV_HEADS, A_GROUP, HEAD_DIM, d).transpose(0, 2, 1, 3, 4)
    w_oa = w_oa.reshape(n_a, A_Q_W, d).astype(BF16)
    w_kv = w_kv_s.astype(BF16)[None]
    w_qb = (w_q_b * QSCALE).astype(BF16)
    w_ob = w_o_b.astype(BF16)
    w_gu = w_gate_up.astype(BF16)
    w_dn = w_down.astype(BF16)

    cak_t = cache_a_k.transpose(0, 1, 3, 4, 2)
    cav_t = cache_a_v.transpose(0, 1, 3, 4, 2)
    cbk_t = cache_b_k.transpose(0, 2, 3, 1)
    cbv_t = cache_b_v.transpose(0, 2, 3, 1)

    units_a = _units_a()
    a_k_p, a_v_p = [], []
    a_cache = None
    qkv_outs = ((0, A_Q_W, 1, BF16), (A_Q_W, A_KV_W, 1, F32), (A_Q_W + A_KV_W, A_KV_W, 1, F32))
    qkv_outs_s = ((0, A_Q_W, 1, F32),) + qkv_outs[1:]
    qb_outs = tuple((gi * B_W, B_W, r, BF16) for gi, (_, r) in enumerate(B_PATTERNS))
    qb_outs_s = tuple((gi * B_W, B_W, 1, F32) for gi in range(B_N_GROUPS))
    kv_outs = [(0, B_W, -wb_p, F32), (B_W, B_W, -wb_p, F32)]
    for _, r in B_PATTERNS:
        kv_outs += [(0, B_W, r, BF16), (B_W, B_W, r, BF16)]
    kv_outs_s = [(0, B_W, 1, F32), (B_W, B_W, 1, F32)]

    def flat(x):
        return x.reshape(-1, x.shape[-1])

    for l in range(depth):
        last = l == depth - 1
        gfin = g_final if last else None
        if l < n_a:
            qp, kp, vp = norm_mm(xp, g_attn[l], w_qkv, l, qkv_outs)
            qs, ks_, vs_ = norm_mm(xs, g_attn[l], w_qkv, l, qkv_outs_s)
            op = band_attn(qp[:, None], kp[:, None], vp[:, None], sinks_a[l], units_a, 1,
                           A_WINDOW - 1, False, unroll=16)
            osm, ko, vo = sample_attn_a(sinks_a[l], flat(qs), flat(ks_), flat(vs_), cak_t, cav_t, l, ts,
                                        prev=a_cache)
            a_cache = (ko, vo)
            a_k_p.append(kp[:, -wa_p:].reshape(bp, wa_p, A_KV_HEADS, HEAD_DIM))
            a_v_p.append(vp[:, -wa_p:].reshape(bp, wa_p, A_KV_HEADS, HEAD_DIM))
            xp = proj_ffn(flat(xp), flat(op), None, w_oa, l, g_ffn[l], w_gu, w_dn, l, gfin, tm=512)
            xs = proj_ffn(flat(xs), osm, None, w_oa, l, g_ffn[l], w_gu, w_dn, l, gfin)
            xp = xp.reshape(bp, tp, d)
            xs = xs.reshape(1, bs * ts, d)
        else:
            b = l - n_a
            if l == n_a:
                kv = norm_mm(xp, g_kv, w_kv, 0, kv_outs)
                kbp, vbp = kv[0], kv[1]
                k_sub, v_sub = kv[2::2], kv[3::2]
                kbs, vbs = norm_mm(xs, g_kv, w_kv, 0, kv_outs_s)
                kbs, vbs = flat(kbs), flat(vbs)
            q_groups = norm_mm(xp, g_attn[l], w_qb, b, qb_outs)
            outs, lses = [], []
            for gi, (w, r) in enumerate(B_PATTERNS):
                q_sub = q_groups[gi] if r > 1 else q_groups[gi][:, None]
                k_g = k_sub[gi] if r > 1 else k_sub[gi][:, None]
                v_g = v_sub[gi] if r > 1 else v_sub[gi][:, None]
                o_g, lse_g = band_attn(q_sub, k_g, v_g, None, _units_b(gi), r, w // r, True, unroll=16)
                outs.append(flat(o_g))
                lses.append(lse_g)
            qs_groups = [flat(q) for q in norm_mm(xs, g_attn[l], w_qb, b, qb_outs_s)]
            if l == n_a:
                osm, kcb_t, vcb_t = sample_attn_b(qs_groups, kbs, vbs, cbk_t, cbv_t, ts, True)
            else:
                osm = sample_attn_b(qs_groups, kbs, vbs, cbk_t, cbv_t, ts, False, bb=2)[0]
            xp = proj_ffn(flat(xp), outs, lses, w_ob, b, g_ffn[l], w_gu, w_dn, l, gfin, tm=512)
            xs = proj_ffn(flat(xs), osm, None, w_ob, b, g_ffn[l], w_gu, w_dn, l, gfin)
            xp = xp.reshape(bp, tp, d)
            xs = xs.reshape(1, bs * ts, d)

    y_sample = xs.reshape(bs, ts, d)
    b_k_p = kbp.reshape(bp, wb_p, B_SLOTS, HEAD_DIM)
    b_v_p = vbp.reshape(bp, wb_p, B_SLOTS, HEAD_DIM)
    return (xp, y_sample,
            jnp.stack(a_k_p, axis=0), jnp.stack(a_v_p, axis=0), b_k_p, b_v_p,
            a_cache[0].transpose(0, 1, 4, 2, 3), a_cache[1].transpose(0, 1, 4, 2, 3),
            kcb_t.transpose(0, 3, 1, 2), vcb_t.transpose(0, 3, 1, 2))
```

```python
import functools
import math

import jax
import jax.numpy as jnp
from jax import lax
from jax.experimental import pallas as pl
from jax.experimental.pallas import tpu as pltpu

D_MODEL = 1024
HEAD_DIM = 64
A_HEADS = 16
A_KV_HEADS = 4
A_GROUP = A_HEADS // A_KV_HEADS
A_WINDOW = 128
A_Q_W = A_HEADS * HEAD_DIM
A_KV_W = A_KV_HEADS * HEAD_DIM
B_SLOTS = 8
B_PATTERNS = ((128, 1), (512, 4), (2048, 16))
B_N_GROUPS = len(B_PATTERNS)
B_HEADS = B_N_GROUPS * B_SLOTS
B_W = B_SLOTS * HEAD_DIM
BLOCK = 128
EPS = 1e-6
NEG = -1e30
LOG2E = math.log2(math.e)
LN2 = math.log(2.0)
QSCALE = HEAD_DIM ** -0.5 * LOG2E

LANES = 128
HALF = LANES // 2
SUPER = 16 * BLOCK
MAX_ROW_STRIDE = 4
VMEM_LIMIT = 56 * 1024 * 1024

F32 = jnp.float32
BF16 = jnp.bfloat16
NT_DIMS = (((1,), (1,)), ((), ()))


def _slopes(n):
    return [float(2.0 ** (-8.0 * (i + 1) / n)) for i in range(n)]


SLOPES_A = _slopes(A_HEADS)
SLOPES_B = _slopes(B_HEADS)


def _rms(x, g):
    ms = jnp.mean(x * x, axis=-1, keepdims=True)
    return (x * lax.rsqrt(ms + EPS)) * g


def _params(n_grid):
    return pltpu.CompilerParams(dimension_semantics=("arbitrary",) * n_grid,
                                vmem_limit_bytes=VMEM_LIMIT)


def _const_spec(shape):
    nd = len(shape)
    return pl.BlockSpec(shape, lambda *_: (0,) * nd, pipeline_mode=pl.Buffered(1))


def _layer_spec(w, l):
    return pl.BlockSpec((None,) + w.shape[1:], lambda *_: (l, 0, 0), pipeline_mode=pl.Buffered(1))


def _norm_mm_kernel(x_ref, *refs, projs):
    n_proj = len(projs)
    n_out = sum(len(outs) for outs in projs)
    out_refs, scr_refs = refs[2 * n_proj:2 * n_proj + n_out], refs[2 * n_proj + n_out:]
    x = x_ref[...]
    xn = x * lax.rsqrt(jnp.mean(x * x, axis=-1, keepdims=True) + EPS)
    oi = si = 0
    for pi, outs in enumerate(projs):
        h = xn * refs[2 * pi][...]
        y = jnp.dot(h.astype(BF16), refs[2 * pi + 1][...], preferred_element_type=F32)
        si = _write_outs(y, outs, out_refs[oi:oi + len(outs)], scr_refs, si)
        oi += len(outs)


def _write_outs(y, outs, out_refs, scr_refs, si):
    for o_ref, (off, width, r, _) in zip(out_refs, outs):
        yo = y[:, off:off + width]
        if r <= 1:
            o_ref[...] = yo.astype(o_ref.dtype)
        else:
            scr, scr_b = scr_refs[si], scr_refs[si + 1]
            si += 2
            tm = yo.shape[0]
            for j in range(width // LANES):
                cols = slice(j * LANES, (j + 1) * LANES)
                scr[j] = yo[:, cols]
                if r <= MAX_ROW_STRIDE:
                    for c in range(r):
                        o_ref[c, :, cols] = scr[j, pl.ds(c, tm // r, stride=r), :].astype(o_ref.dtype)
                else:
                    r1, r2 = MAX_ROW_STRIDE, r // MAX_ROW_STRIDE
                    q = tm // r1
                    for c1 in range(r1):
                        scr_b[j, c1 * q:(c1 + 1) * q, :] = scr[j, pl.ds(c1, q, stride=r1), :]
                    for c1 in range(r1):
                        for c2 in range(r2):
                            o_ref[c2 * r1 + c1, :, cols] = scr_b[
                                j, pl.ds(c1 * q + c2, tm // r, stride=r2), :].astype(o_ref.dtype)
    return si


def norm_mm(x, projs, tm=512):
    bsz, t, d = x.shape
    assert t % tm == 0
    args, in_specs = [x], [pl.BlockSpec((None, tm, d), lambda b, j: (b, j, 0))]
    for g, w, l, _ in projs:
        args += [g.reshape(1, d), w]
        in_specs += [_const_spec((1, d)), _layer_spec(w, l)]
    out_specs, out_shape, scratch = [], [], []
    for (_, width, r, dt) in [o for p in projs for o in p[3]]:
        if r == 1:
            out_specs.append(pl.BlockSpec((None, tm, width), lambda b, j: (b, j, 0)))
            out_shape.append(jax.ShapeDtypeStruct((bsz, t, width), dt))
        elif r < 0:
            tail = -r
            assert tail % tm == 0 and tail <= t
            first = (t - tail) // tm
            out_specs.append(pl.BlockSpec((None, tm, width),
                                          lambda b, j, first=first: (b, jnp.maximum(j - first, 0), 0)))
            out_shape.append(jax.ShapeDtypeStruct((bsz, tail, width), dt))
        else:
            assert (tm // r) % 16 == 0
            out_specs.append(pl.BlockSpec((None, r, tm // r, width), lambda b, j: (b, 0, j, 0)))
            out_shape.append(jax.ShapeDtypeStruct((bsz, r, t // r, width), dt))
            assert r <= MAX_ROW_STRIDE or (r % MAX_ROW_STRIDE == 0 and r // MAX_ROW_STRIDE <= MAX_ROW_STRIDE)
            scratch += [pltpu.VMEM((width // LANES, tm, LANES), F32)] * 2
    return pl.pallas_call(
        functools.partial(_norm_mm_kernel, projs=tuple(tuple(p[3]) for p in projs)),
        grid=(bsz, t // tm),
        in_specs=in_specs,
        out_specs=out_specs,
        out_shape=out_shape,
        scratch_shapes=scratch,
        compiler_params=_params(2),
        name="norm_mm",
    )(*args)


def _proj_ffn_kernel(*refs, combine, final, d_ff):
    it = iter(refs)
    x_ref = next(it)
    if combine:
        o_refs = [next(it) for _ in range(B_N_GROUPS)]
        l_refs = [next(it) for _ in range(B_N_GROUPS)]
    else:
        o_ref = next(it)
    wo_ref, g_ref, wgu_ref, wdn_ref = next(it), next(it), next(it), next(it)
    gfin_ref = next(it) if final else None
    out_ref = next(it)

    if combine:
        ls = [jnp.concatenate([r[j] for j in range(r.shape[0])], axis=1) for r in l_refs]
        mx = jnp.maximum(jnp.maximum(ls[0], ls[1]), ls[2])
        es = [jnp.exp2(l - mx) for l in ls]
        den = es[0] + es[1] + es[2]
        num = es[0] * o_refs[0][...].astype(F32)
        num = num + es[1] * o_refs[1][...].astype(F32)
        num = num + es[2] * o_refs[2][...].astype(F32)
        o = (num / den).astype(BF16)
    else:
        o = o_ref[...].astype(BF16)
    x = x_ref[...] + jnp.dot(o, wo_ref[...], preferred_element_type=F32)
    h = _rms(x, g_ref[...]).astype(BF16)
    gu = jnp.dot(h, wgu_ref[...], preferred_element_type=F32)
    gate = gu[:, :d_ff]
    up = gu[:, d_ff:]
    act = (gate / (1.0 + jnp.exp(-gate))) * up
    y = x + jnp.dot(act.astype(BF16), wdn_ref[...], preferred_element_type=F32)
    if final:
        y = _rms(y, gfin_ref[...])
    out_ref[...] = y


def proj_ffn(x, o, lses, w_o, lo, g, w_gu, w_dn, l, g_final=None, tm=256):
    rows, d = x.shape
    combine = lses is not None
    final = g_final is not None
    d_ff = w_dn.shape[1]
    ow = w_o.shape[1]
    row_spec = lambda w: pl.BlockSpec((tm, w), lambda i: (i, 0))
    args, in_specs = [x], [row_spec(d)]
    if combine:
        args += list(o) + list(lses)
        tpb = lses[0].shape[2] // tm
        in_specs += [row_spec(ow)] * B_N_GROUPS
        in_specs += [pl.BlockSpec((None, ow // LANES, tm, LANES),
                                  lambda i: (i // tpb, 0, i % tpb, 0))] * B_N_GROUPS
    else:
        args.append(o)
        in_specs.append(row_spec(ow))
    args += [w_o, g.reshape(1, d), w_gu, w_dn]
    in_specs += [_layer_spec(w_o, lo), _const_spec((1, d)), _layer_spec(w_gu, l), _layer_spec(w_dn, l)]
    if final:
        args.append(g_final.reshape(1, d))
        in_specs.append(_const_spec((1, d)))
    return pl.pallas_call(
        functools.partial(_proj_ffn_kernel, combine=combine, final=final, d_ff=d_ff),
        grid=(rows // tm,),
        in_specs=in_specs,
        out_specs=row_spec(d),
        out_shape=jax.ShapeDtypeStruct((rows, d), F32),
        compiler_params=_params(1),
        name="proj_ffn",
    )(*args)


def _band_attn_kernel(*refs, units, n_kv_pairs, has_sink, emit_lse, r, max_dist, unroll):
    it = iter(refs)
    sink_ref = next(it) if has_sink else None
    q_ref, kc_ref, kp_ref, vc_ref, vp_ref = (next(it) for _ in range(5))
    o_ref = next(it)
    lse_ref = next(it) if emit_lse else None
    bias_ref = next(it)
    o_scr = next(it) if r > 1 else None
    two_pass = r > MAX_ROW_STRIDE
    r1, r2 = MAX_ROW_STRIDE, r // MAX_ROW_STRIDE
    o_mid = next(it) if two_pass else None
    lse_mid = next(it) if two_pass and emit_lse else None
    assert not two_pass or (SUPER // BLOCK == r and r2 <= MAX_ROW_STRIDE)
    n_blocks = SUPER // BLOCK
    nbeta = n_blocks // r
    step = pl.program_id(1)

    @pl.when((pl.program_id(0) == 0) & (step == 0))
    def _():
        qi = lax.broadcasted_iota(jnp.int32, (2 * BLOCK, 2 * BLOCK), 0) & (BLOCK - 1)
        kj = lax.broadcasted_iota(jnp.int32, (2 * BLOCK, 2 * BLOCK), 1)
        row = lax.broadcasted_iota(jnp.int32, (2 * BLOCK, 2 * BLOCK), 0)
        dist = BLOCK + qi - kj
        valid = (dist >= 0) & (dist <= max_dist)
        distf = (r * dist).astype(F32)
        for u, (_, _, slope_lo, slope_hi, _, _) in enumerate(units):
            slope = jnp.where(row < BLOCK, slope_lo * LOG2E, slope_hi * LOG2E)
            b = -(slope * distf)
            bias_ref[0, u] = jnp.where(valid, b, NEG)
            bias_ref[1, u] = jnp.where(valid & (kj >= BLOCK), b, NEG)

    lane = lax.broadcasted_iota(jnp.int32, (BLOCK, LANES), 1)
    lo = lane < HALF

    def block(blk, carry):
        if nbeta == 1:
            c, beta = blk, 0
        elif r == 1:
            c, beta = 0, blk
        else:
            c = lax.shift_right_logical(blk, nbeta.bit_length() - 1)
            beta = blk & (nbeta - 1)
        if nbeta == 1:
            row0 = prow0 = 0
        else:
            row0 = pl.multiple_of(beta * BLOCK, BLOCK)
            prow0 = pl.multiple_of(jnp.maximum(beta - 1, 0) * BLOCK, BLOCK)
        first_beta = beta == 0
        sel = jnp.where(first_beta & (step == 0), 1, 0)

        def prev_cur(cur_ref, prev_ref, sl):
            cur = cur_ref[c, pl.ds(row0, BLOCK), sl]
            if nbeta == 1:
                prev = prev_ref[c, :, sl]
            else:
                prev = jnp.where(first_beta, prev_ref[c, :, sl], cur_ref[c, pl.ds(prow0, BLOCK), sl])
            return jnp.concatenate([prev, cur], axis=0).astype(BF16)

        kcat, vcat = [], []
        for p in range(n_kv_pairs):
            sl = slice(p * LANES, (p + 1) * LANES)
            kcat.append(prev_cur(kc_ref, kp_ref, sl))
            vcat.append(prev_cur(vc_ref, vp_ref, sl))

        stats = {}
        for u, (qoff, p, _, _, _, _) in enumerate(units):
            q = q_ref[c, pl.ds(row0, BLOCK), qoff:qoff + LANES].astype(F32)
            for h in range(2):
                qh = (jnp.where(lo, q, 0.0) if h == 0 else jnp.where(lo, 0.0, q)).astype(BF16)
                s = lax.dot_general(qh, kcat[p], NT_DIMS, preferred_element_type=F32)
                t = s + bias_ref[sel, u, h * BLOCK:(h + 1) * BLOCK, :]
                m = jnp.max(t, axis=1, keepdims=True)
                e = jnp.exp2(t - m)
                l = jnp.sum(e, axis=1, keepdims=True)
                pv = jnp.dot(e.astype(BF16), vcat[p], preferred_element_type=F32)
                stats[(u, h)] = (m, l, pv)

        for u, (qoff, _, _, _, sink_lo, sink_hi) in enumerate(units):
            cols = slice(qoff, qoff + LANES)
            (m0, l0, pv0), (m1, l1, pv1) = stats[(u, 0)], stats[(u, 1)]
            m_u = jnp.where(lo, m0, m1)
            l_u = jnp.where(lo, l0, l1)
            if has_sink:
                sink_u = jnp.where(lo, sink_ref[sink_lo] * LOG2E, sink_ref[sink_hi] * LOG2E)
                l_u = l_u + jnp.exp2(sink_u - m_u)
            o_sub = jnp.where(lo, pv0, pv1) * (1.0 / l_u)
            if emit_lse:
                lse_sub = m_u + jnp.log2(l_u)
            jq = qoff // LANES
            if r == 1:
                o_ref[pl.ds(row0, BLOCK), cols] = o_sub.astype(o_ref.dtype)
                if emit_lse:
                    lse_ref[jq, pl.ds(row0, BLOCK), :] = lse_sub
            elif not two_pass:
                rows = pl.ds(beta * (BLOCK * r) + c, BLOCK, stride=r)
                o_scr[jq, rows, :] = o_sub
                if emit_lse:
                    lse_ref[jq, rows, :] = lse_sub
            else:
                rows = pl.ds((c & (r1 - 1)) * (SUPER // r1) + lax.shift_right_logical(c, r1.bit_length() - 1),
                             BLOCK, stride=r2)
                o_mid[jq, rows, :] = o_sub
                if emit_lse:
                    lse_mid[jq, rows, :] = lse_sub
        return carry

    lax.fori_loop(0, n_blocks, block, 0, unroll=unroll)
    if r > 1:
        for jq in range(o_scr.shape[0]):
            if two_pass:
                for c1 in range(r1):
                    chunk = slice(c1 * (SUPER // r1), (c1 + 1) * (SUPER // r1))
                    o_scr[jq, pl.ds(c1, SUPER // r1, stride=r1), :] = o_mid[jq, chunk, :]
                    if emit_lse:
                        lse_ref[jq, pl.ds(c1, SUPER // r1, stride=r1), :] = lse_mid[jq, chunk, :]
            o_ref[:, jq * LANES:(jq + 1) * LANES] = o_scr[jq].astype(o_ref.dtype)


def band_attn(q, k, v, sinks, units, r, max_dist, emit_lse, unroll=1):
    bsz, _, tr, qw = q.shape
    t = tr * r
    kw = k.shape[3]
    sr = SUPER // r
    has_sink = sinks is not None
    cur = lambda w: pl.BlockSpec((None, r, sr, w), lambda b, i: (b, 0, i, 0))
    prev = lambda w: pl.BlockSpec((None, r, BLOCK, w),
                                  lambda b, i: (b, 0, jnp.maximum(i * (sr // BLOCK) - 1, 0), 0))
    nat = lambda w: pl.BlockSpec((None, SUPER, w), lambda b, i: (b, i, 0))
    args, in_specs = [], []
    if has_sink:
        args.append(sinks)
        in_specs.append(pl.BlockSpec(memory_space=pltpu.SMEM))
    args += [q, k, k, v, v]
    in_specs += [cur(qw), cur(kw), prev(kw), cur(kw), prev(kw)]
    out_shape = [jax.ShapeDtypeStruct((bsz, t, qw), BF16)]
    out_specs = [nat(qw)]
    if emit_lse:
        out_shape.append(jax.ShapeDtypeStruct((bsz, qw // LANES, t, LANES), F32))
        out_specs.append(pl.BlockSpec((None, qw // LANES, SUPER, LANES), lambda b, i: (b, 0, i, 0)))
    scratch = [pltpu.VMEM((2, len(units), 2 * BLOCK, 2 * BLOCK), F32)]
    if r > 1:
        scratch.append(pltpu.VMEM((qw // LANES, SUPER, LANES), F32))
    if r > MAX_ROW_STRIDE:
        scratch += [pltpu.VMEM((qw // LANES, SUPER, LANES), F32)] * (2 if emit_lse else 1)
    res = pl.pallas_call(
        functools.partial(_band_attn_kernel, units=tuple(units), n_kv_pairs=kw // LANES,
                          has_sink=has_sink, emit_lse=emit_lse, r=r, max_dist=max_dist,
                          unroll=unroll),
        grid=(bsz, t // SUPER),
        in_specs=in_specs,
        out_specs=out_specs,
        out_shape=out_shape,
        scratch_shapes=scratch,
        compiler_params=_params(2),
        name="band_attn",
    )(*args)
    return res if emit_lse else res[0]


def _units_a():
    units = []
    for g in range(A_GROUP):
        for p in range(A_KV_HEADS // 2):
            h_lo = (2 * p) * A_GROUP + g
            h_hi = (2 * p + 1) * A_GROUP + g
            units.append((g * A_KV_W + p * LANES, p, SLOPES_A[h_lo], SLOPES_A[h_hi], h_lo, h_hi))
    return units


def _units_b(gi):
    units = []
    for p in range(B_SLOTS // 2):
        units.append((p * LANES, p, SLOPES_B[gi * B_SLOTS + 2 * p], SLOPES_B[gi * B_SLOTS + 2 * p + 1],
                      0, 0))
    return units


def _lane_group_mask(rows, width, group):
    lane = lax.broadcasted_iota(jnp.int32, (rows, width), 1)
    return (lane >= group * HEAD_DIM) & (lane < (group + 1) * HEAD_DIM)


def _block_diag_rows(q, n_heads):
    s, width = q.shape
    return jnp.concatenate(
        [jnp.where(_lane_group_mask(s, width, h), q, 0.0) for h in range(n_heads)], axis=0)


def _diag_rows(o_bd, n_heads):
    s = o_bd.shape[0] // n_heads
    width = o_bd.shape[1]
    acc = None
    for h in range(n_heads):
        term = jnp.where(_lane_group_mask(s, width, h), o_bd[h * s:(h + 1) * s], 0.0)
        acc = term if acc is None else acc + term
    return acc


def _pad_new(new_rows):
    s, c = new_rows.shape
    return jnp.concatenate([jnp.zeros((LANES - s, c), F32), new_rows], axis=0)


def _shifted(cache_t, new_t, s):
    c, w = cache_t.shape
    rolled = pltpu.roll(cache_t, w - s, 1)
    lane_new = lax.broadcasted_iota(jnp.int32, (c, LANES), 1) >= LANES - s
    last = jnp.where(lane_new, new_t, rolled[:, w - LANES:])
    if w == LANES:
        return last
    return jnp.concatenate([rolled[:, :w - LANES], last], axis=1)


def _sample_a_kernel(*refs, bb, s, layer, n_aliased):
    sink_ref, q_ref, kn_ref, vn_ref, kt_ref, vt_ref = refs[:6]
    o_ref, ko_ref, vo_ref, bias_c_ref, bias_n_ref = refs[6 + n_aliased:]
    if not n_aliased:
        for ll in range(ko_ref.shape[0]):
            if ll != layer:
                ko_ref[ll] = jnp.zeros(ko_ref.shape[1:], F32)
                vo_ref[ll] = jnp.zeros(vo_ref.shape[1:], F32)
        ko_ref, vo_ref = ko_ref.at[layer], vo_ref.at[layer]
    n_rows = A_HEADS * s
    w = kt_ref.shape[-1]
    row1 = lax.broadcasted_iota(jnp.int32, (n_rows, 1), 0)

    @pl.when(pl.program_id(0) == 0)
    def _():
        row = lax.broadcasted_iota(jnp.int32, (n_rows, w), 0)
        tok = row & (s - 1)
        j = lax.broadcasted_iota(jnp.int32, (n_rows, w), 1)
        slope = jnp.zeros((n_rows, w), F32)
        for g in range(A_GROUP):
            for kh in range(A_KV_HEADS):
                slope = jnp.where(row >= (g * A_KV_HEADS + kh) * s,
                                  SLOPES_A[kh * A_GROUP + g] * LOG2E, slope)
        dist_c = w + tok - j
        bias_c_ref[...] = jnp.where((dist_c >= 0) & (dist_c < A_WINDOW),
                                    -(slope * dist_c.astype(F32)), NEG)
        dist_n = tok - (j - (LANES - s))
        bias_n_ref[...] = jnp.where((j >= LANES - s) & (dist_n >= 0) & (dist_n < A_WINDOW),
                                    -(slope * dist_n.astype(F32)), NEG)

    sink = jnp.zeros((n_rows, 1), F32)
    for g in range(A_GROUP):
        for kh in range(A_KV_HEADS):
            sink = jnp.where(row1 >= (g * A_KV_HEADS + kh) * s,
                             sink_ref[kh * A_GROUP + g] * LOG2E, sink)
    bias_c = bias_c_ref[...]
    bias_n = bias_n_ref[...]

    outs = []
    for b in range(bb):
        rows = slice(b * s, (b + 1) * s)
        q = q_ref[rows, :]
        q_bd = jnp.concatenate(
            [_block_diag_rows(q[:, g * A_KV_W:(g + 1) * A_KV_W], A_KV_HEADS) for g in range(A_GROUP)],
            axis=0).astype(BF16)
        kt = kt_ref[b].reshape(A_KV_W, w)
        vt = vt_ref[b].reshape(A_KV_W, w)
        kn = _pad_new(kn_ref[rows, :])
        vn = _pad_new(vn_ref[rows, :])
        tc = jnp.dot(q_bd, kt.astype(BF16), preferred_element_type=F32) + bias_c
        tn = lax.dot_general(q_bd, kn.astype(BF16), NT_DIMS, preferred_element_type=F32) + bias_n
        m = jnp.maximum(jnp.max(tc, axis=1, keepdims=True), jnp.max(tn, axis=1, keepdims=True))
        ec = jnp.exp2(tc - m)
        en = jnp.exp2(tn - m)
        l = (jnp.sum(ec, axis=1, keepdims=True) + jnp.sum(en, axis=1, keepdims=True)
             + jnp.exp2(sink - m))
        inv = 1.0 / l
        o_bd = (lax.dot_general((ec * inv).astype(BF16), vt.astype(BF16), NT_DIMS,
                                preferred_element_type=F32)
                + jnp.dot((en * inv).astype(BF16), vn.astype(BF16), preferred_element_type=F32))
        grp = A_KV_HEADS * s
        outs.append(jnp.concatenate(
            [_diag_rows(o_bd[g * grp:(g + 1) * grp], A_KV_HEADS) for g in range(A_GROUP)], axis=1))
        ko_ref[b] = _shifted(kt, kn.T, s).reshape(ko_ref.shape[1:])
        vo_ref[b] = _shifted(vt, vn.T, s).reshape(vo_ref.shape[1:])
    o_ref[...] = jnp.concatenate(outs, axis=0).astype(o_ref.dtype)


def sample_attn_a(sinks, q, kn, vn, kt, vt, l, s, prev=None, bb=8):
    nl, n, kh, d, w = kt.shape
    rows_spec = lambda width: pl.BlockSpec((bb * s, width), lambda i: (i, 0))
    cache = pl.BlockSpec((None, bb, kh, d, w), lambda i: (l, i, 0, 0, 0))
    args = [sinks, q, kn, vn, kt, vt]
    in_specs = [pl.BlockSpec(memory_space=pltpu.SMEM), rows_spec(A_Q_W), rows_spec(A_KV_W),
                rows_spec(A_KV_W), cache, cache]
    aliases = {}
    cache_out = pl.BlockSpec((nl, bb, kh, d, w), lambda i: (0, i, 0, 0, 0))
    if prev is not None:
        aliases = {len(args): 1, len(args) + 1: 2}
        args += list(prev)
        in_specs += [pl.BlockSpec(memory_space=pl.ANY)] * 2
        cache_out = cache
    return pl.pallas_call(
        functools.partial(_sample_a_kernel, bb=bb, s=s, layer=l, n_aliased=len(aliases)),
        grid=(n // bb,),
        in_specs=in_specs,
        out_specs=[rows_spec(A_Q_W), cache_out, cache_out],
        out_shape=[jax.ShapeDtypeStruct((n * s, A_Q_W), BF16),
                   jax.ShapeDtypeStruct((nl, n, kh, d, w), F32),
                   jax.ShapeDtypeStruct((nl, n, kh, d, w), F32)],
        scratch_shapes=[pltpu.VMEM((A_HEADS * s, w), F32), pltpu.VMEM((A_HEADS * s, LANES), F32)],
        input_output_aliases=aliases,
        compiler_params=_params(1),
        name="sample_attn_a",
    )(*args)


def _sample_b_kernel(*refs, s, bb, emit_cache):
    it = iter(refs)
    q_refs = [next(it) for _ in range(B_N_GROUPS)]
    kn_ref, vn_ref, kt_ref, vt_ref, o_ref = (next(it) for _ in range(5))
    ko_ref, vo_ref = (next(it), next(it)) if emit_cache else (None, None)
    bias_refs = [next(it) for _ in range(B_N_GROUPS)]
    bias_n_ref = next(it)
    w = kt_ref.shape[-1]
    n_rows = B_SLOTS * s
    starts = [max(w - (-(-win // LANES)) * LANES, 0) for win, _ in B_PATTERNS]

    def bias_table(gi, n, dist_of_lane):
        win, r = B_PATTERNS[gi]
        assert r & (r - 1) == 0
        row = lax.broadcasted_iota(jnp.int32, (n_rows, n), 0)
        lane = lax.broadcasted_iota(jnp.int32, (n_rows, n), 1)
        slope = jnp.zeros((n_rows, n), F32)
        for slot in range(B_SLOTS):
            slope = jnp.where(row >= slot * s, SLOPES_B[gi * B_SLOTS + slot] * LOG2E, slope)
        dist, ok = dist_of_lane(row & (s - 1), lane)
        valid = ok & (dist >= 0) & (dist <= win) & ((dist & (r - 1)) == 0)
        return jnp.where(valid, -(slope * dist.astype(F32)), NEG)

    @pl.when(pl.program_id(0) == 0)
    def _():
        for gi in range(B_N_GROUPS):
            bias_refs[gi][...] = bias_table(
                gi, w - starts[gi], lambda tok, lane: (w + tok - (lane + starts[gi]), lane >= 0))
            bias_n_ref[gi] = bias_table(
                gi, LANES, lambda tok, lane: (tok - (lane - (LANES - s)), lane >= LANES - s))

    for b in range(bb):
        _sample_b_one(b, slice(b * s, (b + 1) * s), q_refs, kn_ref, vn_ref, kt_ref, vt_ref, o_ref,
                      ko_ref, vo_ref, bias_refs, bias_n_ref, starts, s)


def _sample_b_one(b, rows, q_refs, kn_ref, vn_ref, kt_ref, vt_ref, o_ref, ko_ref, vo_ref,
                  bias_refs, bias_n_ref, starts, s):
    w = kt_ref.shape[-1]
    kt = kt_ref[b].reshape(B_W, w)
    vt = vt_ref[b].reshape(B_W, w)
    kn = _pad_new(kn_ref[rows, :])
    vn = _pad_new(vn_ref[rows, :])
    kt16 = kt.astype(BF16)
    kn16 = kn.astype(BF16)
    es, ens, ls, lses = [], [], [], []
    for gi in range(B_N_GROUPS):
        q_bd = _block_diag_rows(q_refs[gi][rows, :], B_SLOTS).astype(BF16)
        tc = jnp.dot(q_bd, kt16[:, starts[gi]:], preferred_element_type=F32) + bias_refs[gi][...]
        tn = lax.dot_general(q_bd, kn16, NT_DIMS, preferred_element_type=F32) + bias_n_ref[gi]
        m = jnp.maximum(jnp.max(tc, axis=1, keepdims=True), jnp.max(tn, axis=1, keepdims=True))
        ec = jnp.exp2(tc - m)
        en = jnp.exp2(tn - m)
        l = jnp.sum(ec, axis=1, keepdims=True) + jnp.sum(en, axis=1, keepdims=True)
        es.append(ec)
        ens.append(en)
        ls.append(l)
        lses.append(m + jnp.log2(l))
    mx = jnp.maximum(jnp.maximum(lses[0], lses[1]), lses[2])
    ws = [jnp.exp2(x - mx) for x in lses]
    den = ws[0] + ws[1] + ws[2]
    cs = [ws[gi] / (den * ls[gi]) for gi in range(B_N_GROUPS)]
    tiles = []
    for tix in range(w // LANES):
        acc = None
        for gi in range(B_N_GROUPS):
            off = tix * LANES - starts[gi]
            if off >= 0:
                term = cs[gi] * es[gi][:, off:off + LANES]
                acc = term if acc is None else acc + term
        tiles.append(acc)
    pc = jnp.concatenate(tiles, axis=1).astype(BF16)
    pn = (cs[0] * ens[0] + cs[1] * ens[1] + cs[2] * ens[2]).astype(BF16)
    o_bd = (lax.dot_general(pc, vt.astype(BF16), NT_DIMS, preferred_element_type=F32)
            + jnp.dot(pn, vn.astype(BF16), preferred_element_type=F32))
    o_ref[rows, :] = _diag_rows(o_bd, B_SLOTS)
    if ko_ref is not None:
        ko_ref[b] = _shifted(kt, kn.T, s).reshape(ko_ref.shape[1:])
        vo_ref[b] = _shifted(vt, vn.T, s).reshape(vo_ref.shape[1:])


def sample_attn_b(qs, kn, vn, kt, vt, s, emit_cache, bb=1):
    n, slots, d, w = kt.shape
    rows_spec = pl.BlockSpec((bb * s, B_W), lambda i: (i, 0))
    cache = pl.BlockSpec((bb, slots, d, w), lambda i: (i, 0, 0, 0))
    out_specs = [rows_spec]
    out_shape = [jax.ShapeDtypeStruct((n * s, B_W), F32)]
    if emit_cache:
        out_specs += [cache, cache]
        out_shape += [jax.ShapeDtypeStruct((n, slots, d, w), F32)] * 2
    starts = [max(w - (-(-win // LANES)) * LANES, 0) for win, _ in B_PATTERNS]
    scratch = [pltpu.VMEM((slots * s, w - st), F32) for st in starts]
    scratch.append(pltpu.VMEM((B_N_GROUPS, slots * s, LANES), F32))
    return pl.pallas_call(
        functools.partial(_sample_b_kernel, s=s, bb=bb, emit_cache=emit_cache),
        grid=(n // bb,),
        in_specs=[rows_spec] * (B_N_GROUPS + 2) + [cache, cache],
        out_specs=out_specs,
        out_shape=out_shape,
        scratch_shapes=scratch,
        compiler_params=_params(1),
        name="sample_attn_b",
    )(*qs, kn, vn, kt, vt)


def kernel(x_prompt, x_sample, cache_a_k, cache_a_v, cache_b_k, cache_b_v, g_attn, g_ffn, w_qkv_a,
           sinks_a, w_o_a, g_kv, w_kv_s, w_q_b, w_o_b, w_gate_up, w_down, g_final):
    bp, tp, d = x_prompt.shape
    bs, ts, _ = x_sample.shape
    n_a = w_qkv_a.shape[0]
    depth = g_attn.shape[0]
    wa = cache_a_k.shape[2]
    wb = cache_b_k.shape[1]
    wa_p = min(A_WINDOW, tp)
    wb_p = min(max(w for w, _ in B_PATTERNS), tp)
    assert wa == LANES and wb % LANES == 0 and tp % SUPER == 0

    xp = x_prompt
    xs = x_sample.reshape(1, bs * ts, d)

    wq = w_qkv_a[:, :, :A_Q_W].reshape(n_a, d, A_KV_HEADS, A_GROUP, HEAD_DIM)
    wq = wq.transpose(0, 1, 3, 2, 4).reshape(n_a, d, A_Q_W) * QSCALE
    w_qkv = jnp.concatenate([wq, w_qkv_a[:, :, A_Q_W:]], axis=2).astype(BF16)
    w_oa = w_o_a.reshape(n_a, A_KV_HEADS, A_GROUP, HEAD_DIM, d).transpose(0, 2, 1, 3, 4)
    w_oa = w_oa.reshape(n_a, A_Q_W, d).astype(BF16)
    w_kv = w_kv_s.astype(BF16)[None]
    w_qb = (w_q_b * QSCALE).astype(BF16)
    w_ob = w_o_b.astype(BF16)
    w_gu = w_gate_up.astype(BF16)
    w_dn = w_down.astype(BF16)

    cak_t = cache_a_k.transpose(0, 1, 3, 4, 2)
    cav_t = cache_a_v.transpose(0, 1, 3, 4, 2)
    cbk_t = cache_b_k.transpose(0, 2, 3, 1)
    cbv_t = cache_b_v.transpose(0, 2, 3, 1)

    units_a = _units_a()
    a_k_p, a_v_p = [], []
    a_cache = None
    qkv_outs = ((0, A_Q_W, 1, BF16), (A_Q_W, A_KV_W, 1, F32), (A_Q_W + A_KV_W, A_KV_W, 1, F32))
    qkv_outs_s = ((0, A_Q_W, 1, F32),) + qkv_outs[1:]
    qb_outs = tuple((gi * B_W, B_W, r, BF16) for gi, (_, r) in enumerate(B_PATTERNS))
    qb_outs_s = tuple((gi * B_W, B_W, 1, F32) for gi in range(B_N_GROUPS))
    kv_outs = [(0, B_W, -wb_p, F32), (B_W, B_W, -wb_p, F32)]
    for _, r in B_PATTERNS:
        kv_outs += [(0, B_W, r, BF16), (B_W, B_W, r, BF16)]
    kv_outs_s = [(0, B_W, 1, F32), (B_W, B_W, 1, F32)]

    def flat(x):
        return x.reshape(-1, x.shape[-1])

    for l in range(depth):
        last = l == depth - 1
        gfin = g_final if last else None
        if l < n_a:
            qp, kp, vp = norm_mm(xp, [(g_attn[l], w_qkv, l, qkv_outs)])
            qs, ks_, vs_ = norm_mm(xs, [(g_attn[l], w_qkv, l, qkv_outs_s)])
            op = band_attn(qp[:, None], kp[:, None], vp[:, None], sinks_a[l], units_a, 1,
                           A_WINDOW - 1, False, unroll=16)
            osm, ko, vo = sample_attn_a(sinks_a[l], flat(qs), flat(ks_), flat(vs_), cak_t, cav_t, l, ts,
                                        prev=a_cache)
            a_cache = (ko, vo)
            a_k_p.append(kp[:, -wa_p:].reshape(bp, wa_p, A_KV_HEADS, HEAD_DIM))
            a_v_p.append(vp[:, -wa_p:].reshape(bp, wa_p, A_KV_HEADS, HEAD_DIM))
            xp = proj_ffn(flat(xp), flat(op), None, w_oa, l, g_ffn[l], w_gu, w_dn, l, gfin, tm=512)
            xs = proj_ffn(flat(xs), osm, None, w_oa, l, g_ffn[l], w_gu, w_dn, l, gfin)
            xp = xp.reshape(bp, tp, d)
            xs = xs.reshape(1, bs * ts, d)
        else:
            b = l - n_a
            q_proj = (g_attn[l], w_qb, b, qb_outs)
            q_proj_s = (g_attn[l], w_qb, b, qb_outs_s)
            if l == n_a:
                res = norm_mm(xp, [(g_kv, w_kv, 0, kv_outs), q_proj])
                kv, q_groups = res[:len(kv_outs)], res[len(kv_outs):]
                kbp, vbp = kv[0], kv[1]
                k_sub, v_sub = kv[2::2], kv[3::2]
                res = norm_mm(xs, [(g_kv, w_kv, 0, kv_outs_s), q_proj_s])
                kbs, vbs = flat(res[0]), flat(res[1])
                qs_groups = [flat(q) for q in res[2:]]
            else:
                q_groups = norm_mm(xp, [q_proj])
                qs_groups = [flat(q) for q in norm_mm(xs, [q_proj_s])]
            outs, lses = [], []
            for gi, (w, r) in enumerate(B_PATTERNS):
                q_sub = q_groups[gi] if r > 1 else q_groups[gi][:, None]
                k_g = k_sub[gi] if r > 1 else k_sub[gi][:, None]
                v_g = v_sub[gi] if r > 1 else v_sub[gi][:, None]
                o_g, lse_g = band_attn(q_sub, k_g, v_g, None, _units_b(gi), r, w // r, True, unroll=16)
                outs.append(flat(o_g))
                lses.append(lse_g)
            if l == n_a:
                osm, kcb_t, vcb_t = sample_attn_b(qs_groups, kbs, vbs, cbk_t, cbv_t, ts, True)
            else:
                osm = sample_attn_b(qs_groups, kbs, vbs, cbk_t, cbv_t, ts, False, bb=2)[0]
            xp = proj_ffn(flat(xp), outs, lses, w_ob, b, g_ffn[l], w_gu, w_dn, l, gfin, tm=512)
            xs = proj_ffn(flat(xs), osm, None, w_ob, b, g_ffn[l], w_gu, w_dn, l, gfin)
            xp = xp.reshape(bp, tp, d)
            xs = xs.reshape(1, bs * ts, d)

    y_sample = xs.reshape(bs, ts, d)
    b_k_p = kbp.reshape(bp, wb_p, B_SLOTS, HEAD_DIM)
    b_v_p = vbp.reshape(bp, wb_p, B_SLOTS, HEAD_DIM)
    return (xp, y_sample,
            jnp.stack(a_k_p, axis=0), jnp.stack(a_v_p, axis=0), b_k_p, b_v_p,
            a_cache[0].transpose(0, 1, 4, 2, 3), a_cache[1].transpose(0, 1, 4, 2, 3),
            kcb_t.transpose(0, 3, 1, 2), vcb_t.transpose(0, 3, 1, 2))
```

```python
import functools
import math

import jax
import jax.numpy as jnp
from jax import lax
from jax.experimental import pallas as pl
from jax.experimental.pallas import tpu as pltpu

D_MODEL = 1024
HEAD_DIM = 64
A_HEADS = 16
A_KV_HEADS = 4
A_GROUP = A_HEADS // A_KV_HEADS
A_WINDOW = 128
A_Q_W = A_HEADS * HEAD_DIM
A_KV_W = A_KV_HEADS * HEAD_DIM
B_SLOTS = 8
B_PATTERNS = ((128, 1), (512, 4), (2048, 16))
B_N_GROUPS = len(B_PATTERNS)
B_HEADS = B_N_GROUPS * B_SLOTS
B_W = B_SLOTS * HEAD_DIM
BLOCK = 128
EPS = 1e-6
NEG = -1e30
LOG2E = math.log2(math.e)
LN2 = math.log(2.0)
QSCALE = HEAD_DIM ** -0.5 * LOG2E

LANES = 128
HALF = LANES // 2
SUPER = 16 * BLOCK
MAX_ROW_STRIDE = 4
VMEM_LIMIT = 56 * 1024 * 1024

F32 = jnp.float32
BF16 = jnp.bfloat16
NT_DIMS = (((1,), (1,)), ((), ()))


def _slopes(n):
    return [float(2.0 ** (-8.0 * (i + 1) / n)) for i in range(n)]


SLOPES_A = _slopes(A_HEADS)
SLOPES_B = _slopes(B_HEADS)


def _rms(x, g):
    ms = jnp.mean(x * x, axis=-1, keepdims=True)
    return (x * lax.rsqrt(ms + EPS)) * g


def _params(n_grid):
    return pltpu.CompilerParams(dimension_semantics=("arbitrary",) * n_grid,
                                vmem_limit_bytes=VMEM_LIMIT)


def _const_spec(shape):
    nd = len(shape)
    return pl.BlockSpec(shape, lambda *_: (0,) * nd, pipeline_mode=pl.Buffered(1))


def _layer_spec(w, l):
    return pl.BlockSpec((None,) + w.shape[1:], lambda *_: (l, 0, 0), pipeline_mode=pl.Buffered(1))


def _norm_mm_kernel(x_ref, *refs, projs):
    n_proj = len(projs)
    n_out = sum(len(outs) for outs in projs)
    out_refs, scr_refs = refs[2 * n_proj:2 * n_proj + n_out], refs[2 * n_proj + n_out:]
    x = x_ref[...]
    xn = x * lax.rsqrt(jnp.mean(x * x, axis=-1, keepdims=True) + EPS)
    oi = si = 0
    for pi, outs in enumerate(projs):
        h = xn * refs[2 * pi][...]
        y = jnp.dot(h.astype(BF16), refs[2 * pi + 1][...], preferred_element_type=F32)
        si = _write_outs(y, outs, out_refs[oi:oi + len(outs)], scr_refs, si)
        oi += len(outs)


def _write_outs(y, outs, out_refs, scr_refs, si):
    for o_ref, (off, width, r, _) in zip(out_refs, outs):
        yo = y[:, off:off + width]
        if r <= 1:
            o_ref[...] = yo.astype(o_ref.dtype)
        else:
            scr, scr_b = scr_refs[si], scr_refs[si + 1]
            si += 2
            tm = yo.shape[0]
            for j in range(width // LANES):
                cols = slice(j * LANES, (j + 1) * LANES)
                scr[j] = yo[:, cols]
                if r <= MAX_ROW_STRIDE:
                    for c in range(r):
                        o_ref[c, :, cols] = scr[j, pl.ds(c, tm // r, stride=r), :].astype(o_ref.dtype)
                else:
                    r1, r2 = MAX_ROW_STRIDE, r // MAX_ROW_STRIDE
                    q = tm // r1
                    for c1 in range(r1):
                        scr_b[j, c1 * q:(c1 + 1) * q, :] = scr[j, pl.ds(c1, q, stride=r1), :]
                    for c1 in range(r1):
                        for c2 in range(r2):
                            o_ref[c2 * r1 + c1, :, cols] = scr_b[
                                j, pl.ds(c1 * q + c2, tm // r, stride=r2), :].astype(o_ref.dtype)
    return si


def norm_mm(x, projs, tm=512):
    bsz, t, d = x.shape
    assert t % tm == 0
    args, in_specs = [x], [pl.BlockSpec((None, tm, d), lambda b, j: (b, j, 0))]
    for g, w, l, _ in projs:
        args += [g.reshape(1, d), w]
        in_specs += [_const_spec((1, d)), _layer_spec(w, l)]
    out_specs, out_shape, scratch = [], [], []
    for (_, width, r, dt) in [o for p in projs for o in p[3]]:
        if r == 1:
            out_specs.append(pl.BlockSpec((None, tm, width), lambda b, j: (b, j, 0)))
            out_shape.append(jax.ShapeDtypeStruct((bsz, t, width), dt))
        elif r < 0:
            tail = -r
            assert tail % tm == 0 and tail <= t
            first = (t - tail) // tm
            out_specs.append(pl.BlockSpec((None, tm, width),
                                          lambda b, j, first=first: (b, jnp.maximum(j - first, 0), 0)))
            out_shape.append(jax.ShapeDtypeStruct((bsz, tail, width), dt))
        else:
            assert (tm // r) % 16 == 0
            out_specs.append(pl.BlockSpec((None, r, tm // r, width), lambda b, j: (b, 0, j, 0)))
            out_shape.append(jax.ShapeDtypeStruct((bsz, r, t // r, width), dt))
            assert r <= MAX_ROW_STRIDE or (r % MAX_ROW_STRIDE == 0 and r // MAX_ROW_STRIDE <= MAX_ROW_STRIDE)
            scratch += [pltpu.VMEM((width // LANES, tm, LANES), F32)] * 2
    return pl.pallas_call(
        functools.partial(_norm_mm_kernel, projs=tuple(tuple(p[3]) for p in projs)),
        grid=(bsz, t // tm),
        in_specs=in_specs,
        out_specs=out_specs,
        out_shape=out_shape,
        scratch_shapes=scratch,
        compiler_params=_params(2),
        name="norm_mm",
    )(*args)


def _proj_ffn_kernel(*refs, combine, final, d_ff):
    it = iter(refs)
    x_ref = next(it)
    if combine:
        o_refs = [next(it) for _ in range(B_N_GROUPS)]
        l_refs = [next(it) for _ in range(B_N_GROUPS)]
    else:
        o_ref = next(it)
    wo_ref, g_ref, wgu_ref, wdn_ref = next(it), next(it), next(it), next(it)
    gfin_ref = next(it) if final else None
    out_ref = next(it)

    if combine:
        ls = [jnp.concatenate([r[j] for j in range(r.shape[0])], axis=1) for r in l_refs]
        mx = jnp.maximum(jnp.maximum(ls[0], ls[1]), ls[2])
        es = [jnp.exp2(l - mx) for l in ls]
        den = es[0] + es[1] + es[2]
        num = es[0] * o_refs[0][...].astype(F32)
        num = num + es[1] * o_refs[1][...].astype(F32)
        num = num + es[2] * o_refs[2][...].astype(F32)
        o = (num / den).astype(BF16)
    else:
        o = o_ref[...].astype(BF16)
    x = x_ref[...] + jnp.dot(o, wo_ref[...], preferred_element_type=F32)
    h = _rms(x, g_ref[...]).astype(BF16)
    gu = jnp.dot(h, wgu_ref[...], preferred_element_type=F32)
    gate = gu[:, :d_ff]
    up = gu[:, d_ff:]
    act = (gate / (1.0 + jnp.exp(-gate))) * up
    y = x + jnp.dot(act.astype(BF16), wdn_ref[...], preferred_element_type=F32)
    if final:
        y = _rms(y, gfin_ref[...])
    out_ref[...] = y


def proj_ffn(x, o, lses, w_o, lo, g, w_gu, w_dn, l, g_final=None, tm=256):
    rows, d = x.shape
    combine = lses is not None
    final = g_final is not None
    d_ff = w_dn.shape[1]
    ow = w_o.shape[1]
    row_spec = lambda w: pl.BlockSpec((tm, w), lambda i: (i, 0))
    args, in_specs = [x], [row_spec(d)]
    if combine:
        args += list(o) + list(lses)
        tpb = lses[0].shape[2] // tm
        in_specs += [row_spec(ow)] * B_N_GROUPS
        in_specs += [pl.BlockSpec((None, ow // LANES, tm, LANES),
                                  lambda i: (i // tpb, 0, i % tpb, 0))] * B_N_GROUPS
    else:
        args.append(o)
        in_specs.append(row_spec(ow))
    args += [w_o, g.reshape(1, d), w_gu, w_dn]
    in_specs += [_layer_spec(w_o, lo), _const_spec((1, d)), _layer_spec(w_gu, l), _layer_spec(w_dn, l)]
    if final:
        args.append(g_final.reshape(1, d))
        in_specs.append(_const_spec((1, d)))
    return pl.pallas_call(
        functools.partial(_proj_ffn_kernel, combine=combine, final=final, d_ff=d_ff),
        grid=(rows // tm,),
        in_specs=in_specs,
        out_specs=row_spec(d),
        out_shape=jax.ShapeDtypeStruct((rows, d), F32),
        compiler_params=_params(1),
        name="proj_ffn",
    )(*args)


def _band_attn_kernel(*refs, units, n_kv_pairs, has_sink, emit_lse, r, max_dist, unroll):
    it = iter(refs)
    sink_ref = next(it) if has_sink else None
    q_ref, kc_ref, kp_ref, vc_ref, vp_ref = (next(it) for _ in range(5))
    o_ref = next(it)
    lse_ref = next(it) if emit_lse else None
    bias_ref = next(it)
    o_scr = next(it) if r > 1 else None
    two_pass = r > MAX_ROW_STRIDE
    r1, r2 = MAX_ROW_STRIDE, r // MAX_ROW_STRIDE
    o_mid = next(it) if two_pass else None
    lse_mid = next(it) if two_pass and emit_lse else None
    assert not two_pass or (SUPER // BLOCK == r and r2 <= MAX_ROW_STRIDE)
    n_blocks = SUPER // BLOCK
    nbeta = n_blocks // r
    step = pl.program_id(1)

    @pl.when((pl.program_id(0) == 0) & (step == 0))
    def _():
        qi = lax.broadcasted_iota(jnp.int32, (2 * BLOCK, 2 * BLOCK), 0) & (BLOCK - 1)
        kj = lax.broadcasted_iota(jnp.int32, (2 * BLOCK, 2 * BLOCK), 1)
        row = lax.broadcasted_iota(jnp.int32, (2 * BLOCK, 2 * BLOCK), 0)
        dist = BLOCK + qi - kj
        valid = (dist >= 0) & (dist <= max_dist)
        distf = (r * dist).astype(F32)
        for u, (_, _, slope_lo, slope_hi, _, _) in enumerate(units):
            slope = jnp.where(row < BLOCK, slope_lo * LOG2E, slope_hi * LOG2E)
            b = -(slope * distf)
            bias_ref[0, u] = jnp.where(valid, b, NEG)
            bias_ref[1, u] = jnp.where(valid & (kj >= BLOCK), b, NEG)

    lane = lax.broadcasted_iota(jnp.int32, (BLOCK, LANES), 1)
    lo = lane < HALF

    def block(blk, carry):
        if nbeta == 1:
            c, beta = blk, 0
        elif r == 1:
            c, beta = 0, blk
        else:
            c = lax.shift_right_logical(blk, nbeta.bit_length() - 1)
            beta = blk & (nbeta - 1)
        if nbeta == 1:
            row0 = prow0 = 0
        else:
            row0 = pl.multiple_of(beta * BLOCK, BLOCK)
            prow0 = pl.multiple_of(jnp.maximum(beta - 1, 0) * BLOCK, BLOCK)
        first_beta = beta == 0
        sel = jnp.where(first_beta & (step == 0), 1, 0)

        def prev_cur(cur_ref, prev_ref, sl):
            cur = cur_ref[c, pl.ds(row0, BLOCK), sl]
            if nbeta == 1:
                prev = prev_ref[c, :, sl]
            else:
                prev = jnp.where(first_beta, prev_ref[c, :, sl], cur_ref[c, pl.ds(prow0, BLOCK), sl])
            return jnp.concatenate([prev, cur], axis=0).astype(BF16)

        kcat, vcat = [], []
        for p in range(n_kv_pairs):
            sl = slice(p * LANES, (p + 1) * LANES)
            kcat.append(prev_cur(kc_ref, kp_ref, sl))
            vcat.append(prev_cur(vc_ref, vp_ref, sl))

        stats = {}
        for u, (qoff, p, _, _, _, _) in enumerate(units):
            q = q_ref[c, pl.ds(row0, BLOCK), qoff:qoff + LANES].astype(F32)
            for h in range(2):
                qh = (jnp.where(lo, q, 0.0) if h == 0 else jnp.where(lo, 0.0, q)).astype(BF16)
                s = lax.dot_general(qh, kcat[p], NT_DIMS, preferred_element_type=F32)
                t = s + bias_ref[sel, u, h * BLOCK:(h + 1) * BLOCK, :]
                m = jnp.max(t, axis=1, keepdims=True)
                e = jnp.exp2(t - m)
                l = jnp.sum(e, axis=1, keepdims=True)
                pv = jnp.dot(e.astype(BF16), vcat[p], preferred_element_type=F32)
                stats[(u, h)] = (m, l, pv)

        for u, (qoff, _, _, _, sink_lo, sink_hi) in enumerate(units):
            cols = slice(qoff, qoff + LANES)
            (m0, l0, pv0), (m1, l1, pv1) = stats[(u, 0)], stats[(u, 1)]
            m_u = jnp.where(lo, m0, m1)
            l_u = jnp.where(lo, l0, l1)
            if has_sink:
                sink_u = jnp.where(lo, sink_ref[sink_lo] * LOG2E, sink_ref[sink_hi] * LOG2E)
                l_u = l_u + jnp.exp2(sink_u - m_u)
            o_sub = jnp.where(lo, pv0, pv1) * (1.0 / l_u)
            if emit_lse:
                lse_sub = m_u + jnp.log2(l_u)
            jq = qoff // LANES
            if r == 1:
                o_ref[pl.ds(row0, BLOCK), cols] = o_sub.astype(o_ref.dtype)
                if emit_lse:
                    lse_ref[jq, pl.ds(row0, BLOCK), :] = lse_sub
            elif not two_pass:
                rows = pl.ds(beta * (BLOCK * r) + c, BLOCK, stride=r)
                o_scr[jq, rows, :] = o_sub
                if emit_lse:
                    lse_ref[jq, rows, :] = lse_sub
            else:
                rows = pl.ds((c & (r1 - 1)) * (SUPER // r1) + lax.shift_right_logical(c, r1.bit_length() - 1),
                             BLOCK, stride=r2)
                o_mid[jq, rows, :] = o_sub
                if emit_lse:
                    lse_mid[jq, rows, :] = lse_sub
        return carry

    lax.fori_loop(0, n_blocks, block, 0, unroll=unroll)
    if r > 1:
        for jq in range(o_scr.shape[0]):
            if two_pass:
                for c1 in range(r1):
                    chunk = slice(c1 * (SUPER // r1), (c1 + 1) * (SUPER // r1))
                    o_scr[jq, pl.ds(c1, SUPER // r1, stride=r1), :] = o_mid[jq, chunk, :]
                    if emit_lse:
                        lse_ref[jq, pl.ds(c1, SUPER // r1, stride=r1), :] = lse_mid[jq, chunk, :]
            o_ref[:, jq * LANES:(jq + 1) * LANES] = o_scr[jq].astype(o_ref.dtype)


def band_attn(q, k, v, sinks, units, r, max_dist, emit_lse, unroll=1):
    bsz, _, tr, qw = q.shape
    t = tr * r
    kw = k.shape[3]
    sr = SUPER // r
    has_sink = sinks is not None
    cur = lambda w: pl.BlockSpec((None, r, sr, w), lambda b, i: (b, 0, i, 0))
    prev = lambda w: pl.BlockSpec((None, r, BLOCK, w),
                                  lambda b, i: (b, 0, jnp.maximum(i * (sr // BLOCK) - 1, 0), 0))
    nat = lambda w: pl.BlockSpec((None, SUPER, w), lambda b, i: (b, i, 0))
    args, in_specs = [], []
    if has_sink:
        args.append(sinks)
        in_specs.append(pl.BlockSpec(memory_space=pltpu.SMEM))
    args += [q, k, k, v, v]
    in_specs += [cur(qw), cur(kw), prev(kw), cur(kw), prev(kw)]
    out_shape = [jax.ShapeDtypeStruct((bsz, t, qw), BF16)]
    out_specs = [nat(qw)]
    if emit_lse:
        out_shape.append(jax.ShapeDtypeStruct((bsz, qw // LANES, t, LANES), F32))
        out_specs.append(pl.BlockSpec((None, qw // LANES, SUPER, LANES), lambda b, i: (b, 0, i, 0)))
    scratch = [pltpu.VMEM((2, len(units), 2 * BLOCK, 2 * BLOCK), F32)]
    if r > 1:
        scratch.append(pltpu.VMEM((qw // LANES, SUPER, LANES), F32))
    if r > MAX_ROW_STRIDE:
        scratch += [pltpu.VMEM((qw // LANES, SUPER, LANES), F32)] * (2 if emit_lse else 1)
    res = pl.pallas_call(
        functools.partial(_band_attn_kernel, units=tuple(units), n_kv_pairs=kw // LANES,
                          has_sink=has_sink, emit_lse=emit_lse, r=r, max_dist=max_dist,
                          unroll=unroll),
        grid=(bsz, t // SUPER),
        in_specs=in_specs,
        out_specs=out_specs,
        out_shape=out_shape,
        scratch_shapes=scratch,
        compiler_params=_params(2),
        name="band_attn",
    )(*args)
    return res if emit_lse else res[0]


def _units_a():
    units = []
    for g in range(A_GROUP):
        for p in range(A_KV_HEADS // 2):
            h_lo = (2 * p) * A_GROUP + g
            h_hi = (2 * p + 1) * A_GROUP + g
            units.append((g * A_KV_W + p * LANES, p, SLOPES_A[h_lo], SLOPES_A[h_hi], h_lo, h_hi))
    return units


def _units_b(gi):
    units = []
    for p in range(B_SLOTS // 2):
        units.append((p * LANES, p, SLOPES_B[gi * B_SLOTS + 2 * p], SLOPES_B[gi * B_SLOTS + 2 * p + 1],
                      0, 0))
    return units


def _lane_group_mask(rows, width, group):
    lane = lax.broadcasted_iota(jnp.int32, (rows, width), 1)
    return (lane >= group * HEAD_DIM) & (lane < (group + 1) * HEAD_DIM)


def _block_diag_rows(q, n_heads):
    s, width = q.shape
    return jnp.concatenate(
        [jnp.where(_lane_group_mask(s, width, h), q, 0.0) for h in range(n_heads)], axis=0)


def _diag_rows(o_bd, n_heads):
    s = o_bd.shape[0] // n_heads
    width = o_bd.shape[1]
    acc = None
    for h in range(n_heads):
        term = jnp.where(_lane_group_mask(s, width, h), o_bd[h * s:(h + 1) * s], 0.0)
        acc = term if acc is None else acc + term
    return acc


def _pad_new(new_rows):
    s, c = new_rows.shape
    return jnp.concatenate([jnp.zeros((LANES - s, c), F32), new_rows], axis=0)


def _shifted(cache_t, new_t, s):
    c, w = cache_t.shape
    rolled = pltpu.roll(cache_t, w - s, 1)
    lane_new = lax.broadcasted_iota(jnp.int32, (c, LANES), 1) >= LANES - s
    last = jnp.where(lane_new, new_t, rolled[:, w - LANES:])
    if w == LANES:
        return last
    return jnp.concatenate([rolled[:, :w - LANES], last], axis=1)


def _sample_a_kernel(*refs, bb, s, layer, n_aliased):
    sink_ref, q_ref, kn_ref, vn_ref, kt_ref, vt_ref = refs[:6]
    o_ref, ko_ref, vo_ref, bias_c_ref, bias_n_ref = refs[6 + n_aliased:]
    if not n_aliased:
        for ll in range(ko_ref.shape[0]):
            if ll != layer:
                ko_ref[ll] = jnp.zeros(ko_ref.shape[1:], F32)
                vo_ref[ll] = jnp.zeros(vo_ref.shape[1:], F32)
        ko_ref, vo_ref = ko_ref.at[layer], vo_ref.at[layer]
    n_rows = A_HEADS * s
    w = kt_ref.shape[-1]
    row1 = lax.broadcasted_iota(jnp.int32, (n_rows, 1), 0)

    @pl.when(pl.program_id(0) == 0)
    def _():
        row = lax.broadcasted_iota(jnp.int32, (n_rows, w), 0)
        tok = row & (s - 1)
        j = lax.broadcasted_iota(jnp.int32, (n_rows, w), 1)
        slope = jnp.zeros((n_rows, w), F32)
        for g in range(A_GROUP):
            for kh in range(A_KV_HEADS):
                slope = jnp.where(row >= (g * A_KV_HEADS + kh) * s,
                                  SLOPES_A[kh * A_GROUP + g] * LOG2E, slope)
        dist_c = w + tok - j
        bias_c_ref[...] = jnp.where((dist_c >= 0) & (dist_c < A_WINDOW),
                                    -(slope * dist_c.astype(F32)), NEG)
        dist_n = tok - (j - (LANES - s))
        bias_n_ref[...] = jnp.where((j >= LANES - s) & (dist_n >= 0) & (dist_n < A_WINDOW),
                                    -(slope * dist_n.astype(F32)), NEG)

    sink = jnp.zeros((n_rows, 1), F32)
    for g in range(A_GROUP):
        for kh in range(A_KV_HEADS):
            sink = jnp.where(row1 >= (g * A_KV_HEADS + kh) * s,
                             sink_ref[kh * A_GROUP + g] * LOG2E, sink)
    bias_c = bias_c_ref[...]
    bias_n = bias_n_ref[...]

    outs = []
    for b in range(bb):
        rows = slice(b * s, (b + 1) * s)
        q = q_ref[rows, :]
        q_bd = jnp.concatenate(
            [_block_diag_rows(q[:, g * A_KV_W:(g + 1) * A_KV_W], A_KV_HEADS) for g in range(A_GROUP)],
            axis=0).astype(BF16)
        kt = kt_ref[b].reshape(A_KV_W, w)
        vt = vt_ref[b].reshape(A_KV_W, w)
        kn = _pad_new(kn_ref[rows, :])
        vn = _pad_new(vn_ref[rows, :])
        tc = jnp.dot(q_bd, kt.astype(BF16), preferred_element_type=F32) + bias_c
        tn = lax.dot_general(q_bd, kn.astype(BF16), NT_DIMS, preferred_element_type=F32) + bias_n
        m = jnp.maximum(jnp.max(tc, axis=1, keepdims=True), jnp.max(tn, axis=1, keepdims=True))
        ec = jnp.exp2(tc - m)
        en = jnp.exp2(tn - m)
        l = (jnp.sum(ec, axis=1, keepdims=True) + jnp.sum(en, axis=1, keepdims=True)
             + jnp.exp2(sink - m))
        inv = 1.0 / l
        o_bd = (lax.dot_general((ec * inv).astype(BF16), vt.astype(BF16), NT_DIMS,
                                preferred_element_type=F32)
                + jnp.dot((en * inv).astype(BF16), vn.astype(BF16), preferred_element_type=F32))
        grp = A_KV_HEADS * s
        outs.append(jnp.concatenate(
            [_diag_rows(o_bd[g * grp:(g + 1) * grp], A_KV_HEADS) for g in range(A_GROUP)], axis=1))
        ko_ref[b] = _shifted(kt, kn.T, s).reshape(ko_ref.shape[1:])
        vo_ref[b] = _shifted(vt, vn.T, s).reshape(vo_ref.shape[1:])
    o_ref[...] = jnp.concatenate(outs, axis=0).astype(o_ref.dtype)


def sample_attn_a(sinks, q, kn, vn, kt, vt, l, s, prev=None, bb=8):
    nl, n, kh, d, w = kt.shape
    rows_spec = lambda width: pl.BlockSpec((bb * s, width), lambda i: (i, 0))
    cache = pl.BlockSpec((None, bb, kh, d, w), lambda i: (l, i, 0, 0, 0))
    args = [sinks, q, kn, vn, kt, vt]
    in_specs = [pl.BlockSpec(memory_space=pltpu.SMEM), rows_spec(A_Q_W), rows_spec(A_KV_W),
                rows_spec(A_KV_W), cache, cache]
    aliases = {}
    cache_out = pl.BlockSpec((nl, bb, kh, d, w), lambda i: (0, i, 0, 0, 0))
    if prev is not None:
        aliases = {len(args): 1, len(args) + 1: 2}
        args += list(prev)
        in_specs += [pl.BlockSpec(memory_space=pl.ANY)] * 2
        cache_out = cache
    return pl.pallas_call(
        functools.partial(_sample_a_kernel, bb=bb, s=s, layer=l, n_aliased=len(aliases)),
        grid=(n // bb,),
        in_specs=in_specs,
        out_specs=[rows_spec(A_Q_W), cache_out, cache_out],
        out_shape=[jax.ShapeDtypeStruct((n * s, A_Q_W), BF16),
                   jax.ShapeDtypeStruct((nl, n, kh, d, w), F32),
                   jax.ShapeDtypeStruct((nl, n, kh, d, w), F32)],
        scratch_shapes=[pltpu.VMEM((A_HEADS * s, w), F32), pltpu.VMEM((A_HEADS * s, LANES), F32)],
        input_output_aliases=aliases,
        compiler_params=_params(1),
        name="sample_attn_a",
    )(*args)


def _sample_b_kernel(*refs, s, bb, emit_cache):
    it = iter(refs)
    q_refs = [next(it) for _ in range(B_N_GROUPS)]
    kn_ref, vn_ref, kt_ref, vt_ref, o_ref = (next(it) for _ in range(5))
    ko_ref, vo_ref = (next(it), next(it)) if emit_cache else (None, None)
    bias_refs = [next(it) for _ in range(B_N_GROUPS)]
    bias_n_ref = next(it)
    w = kt_ref.shape[-1]
    n_rows = B_SLOTS * s
    starts = [max(w - (-(-win // LANES)) * LANES, 0) for win, _ in B_PATTERNS]

    def bias_table(gi, n, dist_of_lane):
        win, r = B_PATTERNS[gi]
        assert r & (r - 1) == 0
        row = lax.broadcasted_iota(jnp.int32, (n_rows, n), 0)
        lane = lax.broadcasted_iota(jnp.int32, (n_rows, n), 1)
        slope = jnp.zeros((n_rows, n), F32)
        for slot in range(B_SLOTS):
            slope = jnp.where(row >= slot * s, SLOPES_B[gi * B_SLOTS + slot] * LOG2E, slope)
        dist, ok = dist_of_lane(row & (s - 1), lane)
        valid = ok & (dist >= 0) & (dist <= win) & ((dist & (r - 1)) == 0)
        return jnp.where(valid, -(slope * dist.astype(F32)), NEG)

    @pl.when(pl.program_id(0) == 0)
    def _():
        for gi in range(B_N_GROUPS):
            bias_refs[gi][...] = bias_table(
                gi, w - starts[gi], lambda tok, lane: (w + tok - (lane + starts[gi]), lane >= 0))
            bias_n_ref[gi] = bias_table(
                gi, LANES, lambda tok, lane: (tok - (lane - (LANES - s)), lane >= LANES - s))

    for b in range(bb):
        _sample_b_one(b, slice(b * s, (b + 1) * s), q_refs, kn_ref, vn_ref, kt_ref, vt_ref, o_ref,
                      ko_ref, vo_ref, bias_refs, bias_n_ref, starts, s)


def _sample_b_one(b, rows, q_refs, kn_ref, vn_ref, kt_ref, vt_ref, o_ref, ko_ref, vo_ref,
                  bias_refs, bias_n_ref, starts, s):
    w = kt_ref.shape[-1]
    kt = kt_ref[b].reshape(B_W, w)
    vt = vt_ref[b].reshape(B_W, w)
    kn = _pad_new(kn_ref[rows, :])
    vn = _pad_new(vn_ref[rows, :])
    kt16 = kt.astype(BF16)
    kn16 = kn.astype(BF16)
    es, ens, ls, lses = [], [], [], []
    for gi in range(B_N_GROUPS):
        q_bd = _block_diag_rows(q_refs[gi][rows, :], B_SLOTS).astype(BF16)
        tc = jnp.dot(q_bd, kt16[:, starts[gi]:], preferred_element_type=F32) + bias_refs[gi][...]
        tn = lax.dot_general(q_bd, kn16, NT_DIMS, preferred_element_type=F32) + bias_n_ref[gi]
        m = jnp.maximum(jnp.max(tc, axis=1, keepdims=True), jnp.max(tn, axis=1, keepdims=True))
        ec = jnp.exp2(tc - m)
        en = jnp.exp2(tn - m)
        l = jnp.sum(ec, axis=1, keepdims=True) + jnp.sum(en, axis=1, keepdims=True)
        es.append(ec)
        ens.append(en)
        ls.append(l)
        lses.append(m + jnp.log2(l))
    mx = jnp.maximum(jnp.maximum(lses[0], lses[1]), lses[2])
    ws = [jnp.exp2(x - mx) for x in lses]
    den = ws[0] + ws[1] + ws[2]
    cs = [ws[gi] / (den * ls[gi]) for gi in range(B_N_GROUPS)]
    tiles = []
    for tix in range(w // LANES):
        acc = None
        for gi in range(B_N_GROUPS):
            off = tix * LANES - starts[gi]
            if off >= 0:
                term = cs[gi] * es[gi][:, off:off + LANES]
                acc = term if acc is None else acc + term
        tiles.append(acc)
    pc = jnp.concatenate(tiles, axis=1).astype(BF16)
    pn = (cs[0] * ens[0] + cs[1] * ens[1] + cs[2] * ens[2]).astype(BF16)
    o_bd = (lax.dot_general(pc, vt.astype(BF16), NT_DIMS, preferred_element_type=F32)
            + jnp.dot(pn, vn.astype(BF16), preferred_element_type=F32))
    o_ref[rows, :] = _diag_rows(o_bd, B_SLOTS)
    if ko_ref is not None:
        ko_ref[b] = _shifted(kt, kn.T, s).reshape(ko_ref.shape[1:])
        vo_ref[b] = _shifted(vt, vn.T, s).reshape(vo_ref.shape[1:])


def sample_attn_b(qs, kn, vn, kt, vt, s, emit_cache, bb=1):
    n, slots, d, w = kt.shape
    rows_spec = pl.BlockSpec((bb * s, B_W), lambda i: (i, 0))
    cache = pl.BlockSpec((bb, slots, d, w), lambda i: (i, 0, 0, 0))
    out_specs = [rows_spec]
    out_shape = [jax.ShapeDtypeStruct((n * s, B_W), F32)]
    if emit_cache:
        out_specs += [cache, cache]
        out_shape += [jax.ShapeDtypeStruct((n, slots, d, w), F32)] * 2
    starts = [max(w - (-(-win // LANES)) * LANES, 0) for win, _ in B_PATTERNS]
    scratch = [pltpu.VMEM((slots * s, w - st), F32) for st in starts]
    scratch.append(pltpu.VMEM((B_N_GROUPS, slots * s, LANES), F32))
    return pl.pallas_call(
        functools.partial(_sample_b_kernel, s=s, bb=bb, emit_cache=emit_cache),
        grid=(n // bb,),
        in_specs=[rows_spec] * (B_N_GROUPS + 2) + [cache, cache],
        out_specs=out_specs,
        out_shape=out_shape,
        scratch_shapes=scratch,
        compiler_params=_params(1),
        name="sample_attn_b",
    )(*qs, kn, vn, kt, vt)


def kernel(x_prompt, x_sample, cache_a_k, cache_a_v, cache_b_k, cache_b_v, g_attn, g_ffn, w_qkv_a,
           sinks_a, w_o_a, g_kv, w_kv_s, w_q_b, w_o_b, w_gate_up, w_down, g_final):
    bp, tp, d = x_prompt.shape
    bs, ts, _ = x_sample.shape
    n_a = w_qkv_a.shape[0]
    depth = g_attn.shape[0]
    wa = cache_a_k.shape[2]
    wb = cache_b_k.shape[1]
    wa_p = min(A_WINDOW, tp)
    wb_p = min(max(w for w, _ in B_PATTERNS), tp)
    assert wa == LANES and wb % LANES == 0 and tp % SUPER == 0

    xp = x_prompt
    xs = x_sample.reshape(1, bs * ts, d)

    wq = w_qkv_a[:, :, :A_Q_W].reshape(n_a, d, A_KV_HEADS, A_GROUP, HEAD_DIM)
    wq = wq.transpose(0, 1, 3, 2, 4).reshape(n_a, d, A_Q_W) * QSCALE
    w_qkv = jnp.concatenate([wq, w_qkv_a[:, :, A_Q_W:]], axis=2).astype(BF16)
    w_oa = w_o_a.reshape(n_a, A_KV_HEADS, A_GROUP, HEAD_DIM, d).transpose(0, 2, 1, 3, 4)
    w_oa = w_oa.reshape(n_a, A_Q_W, d).astype(BF16)
    w_kv = w_kv_s.astype(BF16)[None]
    w_qb = (w_q_b * QSCALE).astype(BF16)
    w_ob = w_o_b.astype(BF16)
    w_gu = w_gate_up.astype(BF16)
    w_dn = w_down.astype(BF16)

    cak_t = cache_a_k.transpose(0, 1, 3, 4, 2)
    cav_t = cache_a_v.transpose(0, 1, 3, 4, 2)
    cbk_t = cache_b_k.transpose(0, 2, 3, 1)
    cbv_t = cache_b_v.transpose(0, 2, 3, 1)

    units_a = _units_a()
    a_k_p, a_v_p = [], []
    a_cache = None
    qkv_outs = ((0, A_Q_W, 1, BF16), (A_Q_W, A_KV_W, 1, F32), (A_Q_W + A_KV_W, A_KV_W, 1, F32))
    qkv_outs_s = ((0, A_Q_W, 1, F32),) + qkv_outs[1:]
    qb_outs = tuple((gi * B_W, B_W, r, BF16) for gi, (_, r) in enumerate(B_PATTERNS))
    qb_outs_s = tuple((gi * B_W, B_W, 1, F32) for gi in range(B_N_GROUPS))
    kv_outs = [(0, B_W, -wb_p, F32), (B_W, B_W, -wb_p, F32)]
    for _, r in B_PATTERNS:
        kv_outs += [(0, B_W, r, BF16), (B_W, B_W, r, BF16)]
    kv_outs_s = [(0, B_W, 1, F32), (B_W, B_W, 1, F32)]

    def flat(x):
        return x.reshape(-1, x.shape[-1])

    for l in range(depth):
        last = l == depth - 1
        gfin = g_final if last else None
        if l < n_a:
            qp, kp, vp = norm_mm(xp, [(g_attn[l], w_qkv, l, qkv_outs)], tm=1024)
            qs, ks_, vs_ = norm_mm(xs, [(g_attn[l], w_qkv, l, qkv_outs_s)])
            op = band_attn(qp[:, None], kp[:, None], vp[:, None], sinks_a[l], units_a, 1,
                           A_WINDOW - 1, False, unroll=16)
            osm, ko, vo = sample_attn_a(sinks_a[l], flat(qs), flat(ks_), flat(vs_), cak_t, cav_t, l, ts,
                                        prev=a_cache)
            a_cache = (ko, vo)
            a_k_p.append(kp[:, -wa_p:].reshape(bp, wa_p, A_KV_HEADS, HEAD_DIM))
            a_v_p.append(vp[:, -wa_p:].reshape(bp, wa_p, A_KV_HEADS, HEAD_DIM))
            xp = proj_ffn(flat(xp), flat(op), None, w_oa, l, g_ffn[l], w_gu, w_dn, l, gfin, tm=512)
            xs = proj_ffn(flat(xs), osm, None, w_oa, l, g_ffn[l], w_gu, w_dn, l, gfin)
            xp = xp.reshape(bp, tp, d)
            xs = xs.reshape(1, bs * ts, d)
        else:
            b = l - n_a
            q_proj = (g_attn[l], w_qb, b, qb_outs)
            q_proj_s = (g_attn[l], w_qb, b, qb_outs_s)
            if l == n_a:
                res = norm_mm(xp, [(g_kv, w_kv, 0, kv_outs), q_proj])
                kv, q_groups = res[:len(kv_outs)], res[len(kv_outs):]
                kbp, vbp = kv[0], kv[1]
                k_sub, v_sub = kv[2::2], kv[3::2]
                res = norm_mm(xs, [(g_kv, w_kv, 0, kv_outs_s), q_proj_s])
                kbs, vbs = flat(res[0]), flat(res[1])
                qs_groups = [flat(q) for q in res[2:]]
            else:
                q_groups = norm_mm(xp, [q_proj], tm=1024)
                qs_groups = [flat(q) for q in norm_mm(xs, [q_proj_s])]
            outs, lses = [], []
            for gi, (w, r) in enumerate(B_PATTERNS):
                q_sub = q_groups[gi] if r > 1 else q_groups[gi][:, None]
                k_g = k_sub[gi] if r > 1 else k_sub[gi][:, None]
                v_g = v_sub[gi] if r > 1 else v_sub[gi][:, None]
                o_g, lse_g = band_attn(q_sub, k_g, v_g, None, _units_b(gi), r, w // r, True, unroll=16)
                outs.append(flat(o_g))
                lses.append(lse_g)
            if l == n_a:
                osm, kcb_t, vcb_t = sample_attn_b(qs_groups, kbs, vbs, cbk_t, cbv_t, ts, True)
            else:
                osm = sample_attn_b(qs_groups, kbs, vbs, cbk_t, cbv_t, ts, False, bb=2)[0]
            xp = proj_ffn(flat(xp), outs, lses, w_ob, b, g_ffn[l], w_gu, w_dn, l, gfin, tm=512)
            xs = proj_ffn(flat(xs), osm, None, w_ob, b, g_ffn[l], w_gu, w_dn, l, gfin)
            xp = xp.reshape(bp, tp, d)
            xs = xs.reshape(1, bs * ts, d)

    y_sample = xs.reshape(bs, ts, d)
    b_k_p = kbp.reshape(bp, wb_p, B_SLOTS, HEAD_DIM)
    b_v_p = vbp.reshape(bp, wb_p, B_SLOTS, HEAD_DIM)
    return (xp, y_sample,
            jnp.stack(a_k_p, axis=0), jnp.stack(a_v_p, axis=0), b_k_p, b_v_p,
            a_cache[0].transpose(0, 1, 4, 2, 3), a_cache[1].transpose(0, 1, 4, 2, 3),
            kcb_t.transpose(0, 3, 1, 2), vcb_t.transpose(0, 3, 1, 2))
```

```python
import functools
import math

import jax
import jax.numpy as jnp
from jax import lax
from jax.experimental import pallas as pl
from jax.experimental.pallas import tpu as pltpu

D_MODEL = 1024
HEAD_DIM = 64
A_HEADS = 16
A_KV_HEADS = 4
A_GROUP = A_HEADS // A_KV_HEADS
A_WINDOW = 128
A_Q_W = A_HEADS * HEAD_DIM
A_KV_W = A_KV_HEADS * HEAD_DIM
B_SLOTS = 8
B_PATTERNS = ((128, 1), (512, 4), (2048, 16))
B_N_GROUPS = len(B_PATTERNS)
B_HEADS = B_N_GROUPS * B_SLOTS
B_W = B_SLOTS * HEAD_DIM
BLOCK = 128
EPS = 1e-6
NEG = -1e30
LOG2E = math.log2(math.e)
LN2 = math.log(2.0)
QSCALE = HEAD_DIM ** -0.5 * LOG2E

LANES = 128
HALF = LANES // 2
SUPER = 16 * BLOCK
MAX_ROW_STRIDE = 4
VMEM_LIMIT = 56 * 1024 * 1024

F32 = jnp.float32
BF16 = jnp.bfloat16
NT_DIMS = (((1,), (1,)), ((), ()))


def _slopes(n):
    return [float(2.0 ** (-8.0 * (i + 1) / n)) for i in range(n)]


SLOPES_A = _slopes(A_HEADS)
SLOPES_B = _slopes(B_HEADS)


def _rms(x, g):
    ms = jnp.mean(x * x, axis=-1, keepdims=True)
    return (x * lax.rsqrt(ms + EPS)) * g


def _params(n_grid):
    return pltpu.CompilerParams(dimension_semantics=("arbitrary",) * n_grid,
                                vmem_limit_bytes=VMEM_LIMIT)


def _const_spec(shape):
    nd = len(shape)
    return pl.BlockSpec(shape, lambda *_: (0,) * nd, pipeline_mode=pl.Buffered(1))


def _layer_spec(w, l):
    return pl.BlockSpec((None,) + w.shape[1:], lambda *_: (l, 0, 0), pipeline_mode=pl.Buffered(1))


def _norm_mm_kernel(x_ref, *refs, projs):
    n_proj = len(projs)
    n_out = sum(len(outs) for outs in projs)
    out_refs, scr_refs = refs[2 * n_proj:2 * n_proj + n_out], refs[2 * n_proj + n_out:]
    x = x_ref[...]
    xn = x * lax.rsqrt(jnp.mean(x * x, axis=-1, keepdims=True) + EPS)
    oi = si = 0
    for pi, outs in enumerate(projs):
        h = xn * refs[2 * pi][...]
        y = jnp.dot(h.astype(BF16), refs[2 * pi + 1][...], preferred_element_type=F32)
        si = _write_outs(y, outs, out_refs[oi:oi + len(outs)], scr_refs, si)
        oi += len(outs)


def _write_outs(y, outs, out_refs, scr_refs, si):
    for o_ref, (off, width, r, _) in zip(out_refs, outs):
        yo = y[:, off:off + width]
        if r <= 1:
            o_ref[...] = yo.astype(o_ref.dtype)
        else:
            scr, scr_b = scr_refs[si], scr_refs[si + 1]
            si += 2
            tm = yo.shape[0]
            for j in range(width // LANES):
                cols = slice(j * LANES, (j + 1) * LANES)
                scr[j] = yo[:, cols]
                if r <= MAX_ROW_STRIDE:
                    for c in range(r):
                        o_ref[c, :, cols] = scr[j, pl.ds(c, tm // r, stride=r), :].astype(o_ref.dtype)
                else:
                    r1, r2 = MAX_ROW_STRIDE, r // MAX_ROW_STRIDE
                    q = tm // r1
                    for c1 in range(r1):
                        scr_b[j, c1 * q:(c1 + 1) * q, :] = scr[j, pl.ds(c1, q, stride=r1), :]
                    for c1 in range(r1):
                        for c2 in range(r2):
                            o_ref[c2 * r1 + c1, :, cols] = scr_b[
                                j, pl.ds(c1 * q + c2, tm // r, stride=r2), :].astype(o_ref.dtype)
    return si


def norm_mm(x, projs, tm=512):
    bsz, t, d = x.shape
    assert t % tm == 0
    args, in_specs = [x], [pl.BlockSpec((None, tm, d), lambda b, j: (b, j, 0))]
    for g, w, l, _ in projs:
        args += [g.reshape(1, d), w]
        in_specs += [_const_spec((1, d)), _layer_spec(w, l)]
    out_specs, out_shape, scratch = [], [], []
    for (_, width, r, dt) in [o for p in projs for o in p[3]]:
        if r == 1:
            out_specs.append(pl.BlockSpec((None, tm, width), lambda b, j: (b, j, 0)))
            out_shape.append(jax.ShapeDtypeStruct((bsz, t, width), dt))
        elif r < 0:
            tail = -r
            assert tail % tm == 0 and tail <= t
            first = (t - tail) // tm
            out_specs.append(pl.BlockSpec((None, tm, width),
                                          lambda b, j, first=first: (b, jnp.maximum(j - first, 0), 0)))
            out_shape.append(jax.ShapeDtypeStruct((bsz, tail, width), dt))
        else:
            assert (tm // r) % 16 == 0
            out_specs.append(pl.BlockSpec((None, r, tm // r, width), lambda b, j: (b, 0, j, 0)))
            out_shape.append(jax.ShapeDtypeStruct((bsz, r, t // r, width), dt))
            assert r <= MAX_ROW_STRIDE or (r % MAX_ROW_STRIDE == 0 and r // MAX_ROW_STRIDE <= MAX_ROW_STRIDE)
            scratch += [pltpu.VMEM((width // LANES, tm, LANES), F32)] * 2
    return pl.pallas_call(
        functools.partial(_norm_mm_kernel, projs=tuple(tuple(p[3]) for p in projs)),
        grid=(bsz, t // tm),
        in_specs=in_specs,
        out_specs=out_specs,
        out_shape=out_shape,
        scratch_shapes=scratch,
        compiler_params=_params(2),
        name="norm_mm",
    )(*args)


def _proj_ffn_kernel(*refs, combine, final, d_ff, n_main):
    it = iter(refs)
    x_ref = next(it)
    if combine:
        o_refs = [next(it) for _ in range(B_N_GROUPS)]
        l_refs = [next(it) for _ in range(B_N_GROUPS)]
    else:
        o_ref = next(it)
    xs_ref, os_ref = (next(it), next(it)) if n_main is not None else (None, None)
    wo_ref, g_ref, wgu_ref, wdn_ref = next(it), next(it), next(it), next(it)
    gfin_ref = next(it) if final else None
    out_ref = next(it)
    outs_ref = next(it) if n_main is not None else None
    extra = (pl.program_id(0) >= n_main) if n_main is not None else None

    if combine:
        ls = [jnp.concatenate([r[j] for j in range(r.shape[0])], axis=1) for r in l_refs]
        mx = jnp.maximum(jnp.maximum(ls[0], ls[1]), ls[2])
        es = [jnp.exp2(l - mx) for l in ls]
        den = es[0] + es[1] + es[2]
        num = es[0] * o_refs[0][...].astype(F32)
        num = num + es[1] * o_refs[1][...].astype(F32)
        num = num + es[2] * o_refs[2][...].astype(F32)
        o = (num / den).astype(BF16)
    else:
        o = o_ref[...].astype(BF16)
    x = x_ref[...]
    if n_main is not None:
        o = jnp.where(extra, os_ref[...].astype(BF16), o)
        x = jnp.where(extra, xs_ref[...], x)
    x = x + jnp.dot(o, wo_ref[...], preferred_element_type=F32)
    h = _rms(x, g_ref[...]).astype(BF16)
    gu = jnp.dot(h, wgu_ref[...], preferred_element_type=F32)
    gate = gu[:, :d_ff]
    up = gu[:, d_ff:]
    act = (gate / (1.0 + jnp.exp(-gate))) * up
    y = x + jnp.dot(act.astype(BF16), wdn_ref[...], preferred_element_type=F32)
    if final:
        y = _rms(y, gfin_ref[...])
    if n_main is None:
        out_ref[...] = y
    else:
        @pl.when(jnp.logical_not(extra))
        def _():
            out_ref[...] = y

        @pl.when(extra)
        def _():
            outs_ref[...] = y


def proj_ffn(x, o, lses, w_o, lo, g, w_gu, w_dn, l, g_final=None, tm=256, xs=None, os_=None):
    rows, d = x.shape
    combine = lses is not None
    final = g_final is not None
    dual = xs is not None
    d_ff = w_dn.shape[1]
    ow = w_o.shape[1]
    n_main = rows // tm
    main = (lambda i: jnp.minimum(i, n_main - 1)) if dual else (lambda i: i)
    row_spec = lambda w: pl.BlockSpec((tm, w), lambda i: (main(i), 0))
    extra_spec = lambda w: pl.BlockSpec((tm, w), lambda i: (jnp.maximum(i - n_main, 0), 0))
    args, in_specs = [x], [row_spec(d)]
    if combine:
        args += list(o) + list(lses)
        tpb = lses[0].shape[2] // tm
        in_specs += [row_spec(ow)] * B_N_GROUPS
        in_specs += [pl.BlockSpec((None, ow // LANES, tm, LANES),
                                  lambda i: (main(i) // tpb, 0, main(i) % tpb, 0))] * B_N_GROUPS
    else:
        args.append(o)
        in_specs.append(row_spec(ow))
    n_steps = n_main
    out_specs, out_shape = row_spec(d), jax.ShapeDtypeStruct((rows, d), F32)
    if dual:
        assert xs.shape[0] % tm == 0
        args += [xs, os_]
        in_specs += [extra_spec(d), extra_spec(ow)]
        n_steps += xs.shape[0] // tm
        out_specs = [out_specs, extra_spec(d)]
        out_shape = [out_shape, jax.ShapeDtypeStruct(xs.shape, F32)]
    args += [w_o, g.reshape(1, d), w_gu, w_dn]
    in_specs += [_layer_spec(w_o, lo), _const_spec((1, d)), _layer_spec(w_gu, l), _layer_spec(w_dn, l)]
    if final:
        args.append(g_final.reshape(1, d))
        in_specs.append(_const_spec((1, d)))
    return pl.pallas_call(
        functools.partial(_proj_ffn_kernel, combine=combine, final=final, d_ff=d_ff,
                          n_main=n_main if dual else None),
        grid=(n_steps,),
        in_specs=in_specs,
        out_specs=out_specs,
        out_shape=out_shape,
        compiler_params=_params(1),
        name="proj_ffn",
    )(*args)


def _band_attn_kernel(*refs, units, n_kv_pairs, has_sink, emit_lse, r, max_dist, unroll):
    it = iter(refs)
    sink_ref = next(it) if has_sink else None
    q_ref, kc_ref, kp_ref, vc_ref, vp_ref = (next(it) for _ in range(5))
    o_ref = next(it)
    lse_ref = next(it) if emit_lse else None
    bias_ref = next(it)
    o_scr = next(it) if r > 1 else None
    two_pass = r > MAX_ROW_STRIDE
    r1, r2 = MAX_ROW_STRIDE, r // MAX_ROW_STRIDE
    o_mid = next(it) if two_pass else None
    lse_mid = next(it) if two_pass and emit_lse else None
    assert not two_pass or (SUPER // BLOCK == r and r2 <= MAX_ROW_STRIDE)
    n_blocks = SUPER // BLOCK
    nbeta = n_blocks // r
    step = pl.program_id(1)

    @pl.when((pl.program_id(0) == 0) & (step == 0))
    def _():
        qi = lax.broadcasted_iota(jnp.int32, (2 * BLOCK, 2 * BLOCK), 0) & (BLOCK - 1)
        kj = lax.broadcasted_iota(jnp.int32, (2 * BLOCK, 2 * BLOCK), 1)
        row = lax.broadcasted_iota(jnp.int32, (2 * BLOCK, 2 * BLOCK), 0)
        dist = BLOCK + qi - kj
        valid = (dist >= 0) & (dist <= max_dist)
        distf = (r * dist).astype(F32)
        for u, (_, _, slope_lo, slope_hi, _, _) in enumerate(units):
            slope = jnp.where(row < BLOCK, slope_lo * LOG2E, slope_hi * LOG2E)
            b = -(slope * distf)
            bias_ref[0, u] = jnp.where(valid, b, NEG)
            bias_ref[1, u] = jnp.where(valid & (kj >= BLOCK), b, NEG)

    lane = lax.broadcasted_iota(jnp.int32, (BLOCK, LANES), 1)
    lo = lane < HALF

    def block(blk, carry):
        if nbeta == 1:
            c, beta = blk, 0
        elif r == 1:
            c, beta = 0, blk
        else:
            c = lax.shift_right_logical(blk, nbeta.bit_length() - 1)
            beta = blk & (nbeta - 1)
        if nbeta == 1:
            row0 = prow0 = 0
        else:
            row0 = pl.multiple_of(beta * BLOCK, BLOCK)
            prow0 = pl.multiple_of(jnp.maximum(beta - 1, 0) * BLOCK, BLOCK)
        first_beta = beta == 0
        sel = jnp.where(first_beta & (step == 0), 1, 0)

        def prev_cur(cur_ref, prev_ref, sl):
            cur = cur_ref[c, pl.ds(row0, BLOCK), sl]
            if nbeta == 1:
                prev = prev_ref[c, :, sl]
            else:
                prev = jnp.where(first_beta, prev_ref[c, :, sl], cur_ref[c, pl.ds(prow0, BLOCK), sl])
            return jnp.concatenate([prev, cur], axis=0).astype(BF16)

        kcat, vcat = [], []
        for p in range(n_kv_pairs):
            sl = slice(p * LANES, (p + 1) * LANES)
            kcat.append(prev_cur(kc_ref, kp_ref, sl))
            vcat.append(prev_cur(vc_ref, vp_ref, sl))

        stats = {}
        for u, (qoff, p, _, _, _, _) in enumerate(units):
            q = q_ref[c, pl.ds(row0, BLOCK), qoff:qoff + LANES].astype(F32)
            for h in range(2):
                qh = (jnp.where(lo, q, 0.0) if h == 0 else jnp.where(lo, 0.0, q)).astype(BF16)
                s = lax.dot_general(qh, kcat[p], NT_DIMS, preferred_element_type=F32)
                t = s + bias_ref[sel, u, h * BLOCK:(h + 1) * BLOCK, :]
                m = jnp.max(t, axis=1, keepdims=True)
                e = jnp.exp2(t - m)
                l = jnp.sum(e, axis=1, keepdims=True)
                pv = jnp.dot(e.astype(BF16), vcat[p], preferred_element_type=F32)
                stats[(u, h)] = (m, l, pv)

        for u, (qoff, _, _, _, sink_lo, sink_hi) in enumerate(units):
            cols = slice(qoff, qoff + LANES)
            (m0, l0, pv0), (m1, l1, pv1) = stats[(u, 0)], stats[(u, 1)]
            m_u = jnp.where(lo, m0, m1)
            l_u = jnp.where(lo, l0, l1)
            if has_sink:
                sink_u = jnp.where(lo, sink_ref[sink_lo] * LOG2E, sink_ref[sink_hi] * LOG2E)
                l_u = l_u + jnp.exp2(sink_u - m_u)
            o_sub = jnp.where(lo, pv0, pv1) * (1.0 / l_u)
            if emit_lse:
                lse_sub = m_u + jnp.log2(l_u)
            jq = qoff // LANES
            if r == 1:
                o_ref[pl.ds(row0, BLOCK), cols] = o_sub.astype(o_ref.dtype)
                if emit_lse:
                    lse_ref[jq, pl.ds(row0, BLOCK), :] = lse_sub
            elif not two_pass:
                rows = pl.ds(beta * (BLOCK * r) + c, BLOCK, stride=r)
                o_scr[jq, rows, :] = o_sub
                if emit_lse:
                    lse_ref[jq, rows, :] = lse_sub
            else:
                rows = pl.ds((c & (r1 - 1)) * (SUPER // r1) + lax.shift_right_logical(c, r1.bit_length() - 1),
                             BLOCK, stride=r2)
                o_mid[jq, rows, :] = o_sub
                if emit_lse:
                    lse_mid[jq, rows, :] = lse_sub
        return carry

    lax.fori_loop(0, n_blocks, block, 0, unroll=unroll)
    if r > 1:
        for jq in range(o_scr.shape[0]):
            if two_pass:
                for c1 in range(r1):
                    chunk = slice(c1 * (SUPER // r1), (c1 + 1) * (SUPER // r1))
                    o_scr[jq, pl.ds(c1, SUPER // r1, stride=r1), :] = o_mid[jq, chunk, :]
                    if emit_lse:
                        lse_ref[jq, pl.ds(c1, SUPER // r1, stride=r1), :] = lse_mid[jq, chunk, :]
            o_ref[:, jq * LANES:(jq + 1) * LANES] = o_scr[jq].astype(o_ref.dtype)


def band_attn(q, k, v, sinks, units, r, max_dist, emit_lse, unroll=1):
    bsz, _, tr, qw = q.shape
    t = tr * r
    kw = k.shape[3]
    sr = SUPER // r
    has_sink = sinks is not None
    cur = lambda w: pl.BlockSpec((None, r, sr, w), lambda b, i: (b, 0, i, 0))
    prev = lambda w: pl.BlockSpec((None, r, BLOCK, w),
                                  lambda b, i: (b, 0, jnp.maximum(i * (sr // BLOCK) - 1, 0), 0))
    nat = lambda w: pl.BlockSpec((None, SUPER, w), lambda b, i: (b, i, 0))
    args, in_specs = [], []
    if has_sink:
        args.append(sinks)
        in_specs.append(pl.BlockSpec(memory_space=pltpu.SMEM))
    args += [q, k, k, v, v]
    in_specs += [cur(qw), cur(kw), prev(kw), cur(kw), prev(kw)]
    out_shape = [jax.ShapeDtypeStruct((bsz, t, qw), BF16)]
    out_specs = [nat(qw)]
    if emit_lse:
        out_shape.append(jax.ShapeDtypeStruct((bsz, qw // LANES, t, LANES), F32))
        out_specs.append(pl.BlockSpec((None, qw // LANES, SUPER, LANES), lambda b, i: (b, 0, i, 0)))
    scratch = [pltpu.VMEM((2, len(units), 2 * BLOCK, 2 * BLOCK), F32)]
    if r > 1:
        scratch.append(pltpu.VMEM((qw // LANES, SUPER, LANES), F32))
    if r > MAX_ROW_STRIDE:
        scratch += [pltpu.VMEM((qw // LANES, SUPER, LANES), F32)] * (2 if emit_lse else 1)
    res = pl.pallas_call(
        functools.partial(_band_attn_kernel, units=tuple(units), n_kv_pairs=kw // LANES,
                          has_sink=has_sink, emit_lse=emit_lse, r=r, max_dist=max_dist,
                          unroll=unroll),
        grid=(bsz, t // SUPER),
        in_specs=in_specs,
        out_specs=out_specs,
        out_shape=out_shape,
        scratch_shapes=scratch,
        compiler_params=_params(2),
        name="band_attn",
    )(*args)
    return res if emit_lse else res[0]


def _units_a():
    units = []
    for g in range(A_GROUP):
        for p in range(A_KV_HEADS // 2):
            h_lo = (2 * p) * A_GROUP + g
            h_hi = (2 * p + 1) * A_GROUP + g
            units.append((g * A_KV_W + p * LANES, p, SLOPES_A[h_lo], SLOPES_A[h_hi], h_lo, h_hi))
    return units


def _units_b(gi):
    units = []
    for p in range(B_SLOTS // 2):
        units.append((p * LANES, p, SLOPES_B[gi * B_SLOTS + 2 * p], SLOPES_B[gi * B_SLOTS + 2 * p + 1],
                      0, 0))
    return units


def _lane_group_mask(rows, width, group):
    lane = lax.broadcasted_iota(jnp.int32, (rows, width), 1)
    return (lane >= group * HEAD_DIM) & (lane < (group + 1) * HEAD_DIM)


def _block_diag_rows(q, n_heads):
    s, width = q.shape
    return jnp.concatenate(
        [jnp.where(_lane_group_mask(s, width, h), q, 0.0) for h in range(n_heads)], axis=0)


def _diag_rows(o_bd, n_heads):
    s = o_bd.shape[0] // n_heads
    width = o_bd.shape[1]
    acc = None
    for h in range(n_heads):
        term = jnp.where(_lane_group_mask(s, width, h), o_bd[h * s:(h + 1) * s], 0.0)
        acc = term if acc is None else acc + term
    return acc


def _pad_new(new_rows):
    s, c = new_rows.shape
    return jnp.concatenate([jnp.zeros((LANES - s, c), F32), new_rows], axis=0)


def _shifted(cache_t, new_t, s):
    c, w = cache_t.shape
    rolled = pltpu.roll(cache_t, w - s, 1)
    lane_new = lax.broadcasted_iota(jnp.int32, (c, LANES), 1) >= LANES - s
    last = jnp.where(lane_new, new_t, rolled[:, w - LANES:])
    if w == LANES:
        return last
    return jnp.concatenate([rolled[:, :w - LANES], last], axis=1)


def _sample_a_kernel(*refs, bb, s, layer, n_aliased):
    sink_ref, q_ref, kn_ref, vn_ref, kt_ref, vt_ref = refs[:6]
    o_ref, ko_ref, vo_ref, bias_c_ref, bias_n_ref = refs[6 + n_aliased:]
    if not n_aliased:
        for ll in range(ko_ref.shape[0]):
            if ll != layer:
                ko_ref[ll] = jnp.zeros(ko_ref.shape[1:], F32)
                vo_ref[ll] = jnp.zeros(vo_ref.shape[1:], F32)
        ko_ref, vo_ref = ko_ref.at[layer], vo_ref.at[layer]
    n_rows = A_HEADS * s
    w = kt_ref.shape[-1]
    row1 = lax.broadcasted_iota(jnp.int32, (n_rows, 1), 0)

    @pl.when(pl.program_id(0) == 0)
    def _():
        row = lax.broadcasted_iota(jnp.int32, (n_rows, w), 0)
        tok = row & (s - 1)
        j = lax.broadcasted_iota(jnp.int32, (n_rows, w), 1)
        slope = jnp.zeros((n_rows, w), F32)
        for g in range(A_GROUP):
            for kh in range(A_KV_HEADS):
                slope = jnp.where(row >= (g * A_KV_HEADS + kh) * s,
                                  SLOPES_A[kh * A_GROUP + g] * LOG2E, slope)
        dist_c = w + tok - j
        bias_c_ref[...] = jnp.where((dist_c >= 0) & (dist_c < A_WINDOW),
                                    -(slope * dist_c.astype(F32)), NEG)
        dist_n = tok - (j - (LANES - s))
        bias_n_ref[...] = jnp.where((j >= LANES - s) & (dist_n >= 0) & (dist_n < A_WINDOW),
                                    -(slope * dist_n.astype(F32)), NEG)

    sink = jnp.zeros((n_rows, 1), F32)
    for g in range(A_GROUP):
        for kh in range(A_KV_HEADS):
            sink = jnp.where(row1 >= (g * A_KV_HEADS + kh) * s,
                             sink_ref[kh * A_GROUP + g] * LOG2E, sink)
    bias_c = bias_c_ref[...]
    bias_n = bias_n_ref[...]

    outs = []
    for b in range(bb):
        rows = slice(b * s, (b + 1) * s)
        q = q_ref[rows, :]
        q_bd = jnp.concatenate(
            [_block_diag_rows(q[:, g * A_KV_W:(g + 1) * A_KV_W], A_KV_HEADS) for g in range(A_GROUP)],
            axis=0).astype(BF16)
        kt = kt_ref[b].reshape(A_KV_W, w)
        vt = vt_ref[b].reshape(A_KV_W, w)
        kn = _pad_new(kn_ref[rows, :])
        vn = _pad_new(vn_ref[rows, :])
        tc = jnp.dot(q_bd, kt.astype(BF16), preferred_element_type=F32) + bias_c
        tn = lax.dot_general(q_bd, kn.astype(BF16), NT_DIMS, preferred_element_type=F32) + bias_n
        m = jnp.maximum(jnp.max(tc, axis=1, keepdims=True), jnp.max(tn, axis=1, keepdims=True))
        ec = jnp.exp2(tc - m)
        en = jnp.exp2(tn - m)
        l = (jnp.sum(ec, axis=1, keepdims=True) + jnp.sum(en, axis=1, keepdims=True)
             + jnp.exp2(sink - m))
        inv = 1.0 / l
        o_bd = (lax.dot_general((ec * inv).astype(BF16), vt.astype(BF16), NT_DIMS,
                                preferred_element_type=F32)
                + jnp.dot((en * inv).astype(BF16), vn.astype(BF16), preferred_element_type=F32))
        grp = A_KV_HEADS * s
        outs.append(jnp.concatenate(
            [_diag_rows(o_bd[g * grp:(g + 1) * grp], A_KV_HEADS) for g in range(A_GROUP)], axis=1))
        ko_ref[b] = _shifted(kt, kn.T, s).reshape(ko_ref.shape[1:])
        vo_ref[b] = _shifted(vt, vn.T, s).reshape(vo_ref.shape[1:])
    o_ref[...] = jnp.concatenate(outs, axis=0).astype(o_ref.dtype)


def sample_attn_a(sinks, q, kn, vn, kt, vt, l, s, prev=None, bb=8):
    nl, n, kh, d, w = kt.shape
    rows_spec = lambda width: pl.BlockSpec((bb * s, width), lambda i: (i, 0))
    cache = pl.BlockSpec((None, bb, kh, d, w), lambda i: (l, i, 0, 0, 0))
    args = [sinks, q, kn, vn, kt, vt]
    in_specs = [pl.BlockSpec(memory_space=pltpu.SMEM), rows_spec(A_Q_W), rows_spec(A_KV_W),
                rows_spec(A_KV_W), cache, cache]
    aliases = {}
    cache_out = pl.BlockSpec((nl, bb, kh, d, w), lambda i: (0, i, 0, 0, 0))
    if prev is not None:
        aliases = {len(args): 1, len(args) + 1: 2}
        args += list(prev)
        in_specs += [pl.BlockSpec(memory_space=pl.ANY)] * 2
        cache_out = cache
    return pl.pallas_call(
        functools.partial(_sample_a_kernel, bb=bb, s=s, layer=l, n_aliased=len(aliases)),
        grid=(n // bb,),
        in_specs=in_specs,
        out_specs=[rows_spec(A_Q_W), cache_out, cache_out],
        out_shape=[jax.ShapeDtypeStruct((n * s, A_Q_W), BF16),
                   jax.ShapeDtypeStruct((nl, n, kh, d, w), F32),
                   jax.ShapeDtypeStruct((nl, n, kh, d, w), F32)],
        scratch_shapes=[pltpu.VMEM((A_HEADS * s, w), F32), pltpu.VMEM((A_HEADS * s, LANES), F32)],
        input_output_aliases=aliases,
        compiler_params=_params(1),
        name="sample_attn_a",
    )(*args)


def _sample_b_kernel(*refs, s, bb, emit_cache):
    it = iter(refs)
    q_refs = [next(it) for _ in range(B_N_GROUPS)]
    kn_ref, vn_ref, kt_ref, vt_ref, o_ref = (next(it) for _ in range(5))
    ko_ref, vo_ref = (next(it), next(it)) if emit_cache else (None, None)
    bias_refs = [next(it) for _ in range(B_N_GROUPS)]
    bias_n_ref = next(it)
    w = kt_ref.shape[-1]
    n_rows = B_SLOTS * s
    starts = [max(w - (-(-win // LANES)) * LANES, 0) for win, _ in B_PATTERNS]

    def bias_table(gi, n, dist_of_lane):
        win, r = B_PATTERNS[gi]
        assert r & (r - 1) == 0
        row = lax.broadcasted_iota(jnp.int32, (n_rows, n), 0)
        lane = lax.broadcasted_iota(jnp.int32, (n_rows, n), 1)
        slope = jnp.zeros((n_rows, n), F32)
        for slot in range(B_SLOTS):
            slope = jnp.where(row >= slot * s, SLOPES_B[gi * B_SLOTS + slot] * LOG2E, slope)
        dist, ok = dist_of_lane(row & (s - 1), lane)
        valid = ok & (dist >= 0) & (dist <= win) & ((dist & (r - 1)) == 0)
        return jnp.where(valid, -(slope * dist.astype(F32)), NEG)

    @pl.when(pl.program_id(0) == 0)
    def _():
        for gi in range(B_N_GROUPS):
            bias_refs[gi][...] = bias_table(
                gi, w - starts[gi], lambda tok, lane: (w + tok - (lane + starts[gi]), lane >= 0))
            bias_n_ref[gi] = bias_table(
                gi, LANES, lambda tok, lane: (tok - (lane - (LANES - s)), lane >= LANES - s))

    for b in range(bb):
        _sample_b_one(b, slice(b * s, (b + 1) * s), q_refs, kn_ref, vn_ref, kt_ref, vt_ref, o_ref,
                      ko_ref, vo_ref, bias_refs, bias_n_ref, starts, s)


def _sample_b_one(b, rows, q_refs, kn_ref, vn_ref, kt_ref, vt_ref, o_ref, ko_ref, vo_ref,
                  bias_refs, bias_n_ref, starts, s):
    w = kt_ref.shape[-1]
    kt = kt_ref[b].reshape(B_W, w)
    vt = vt_ref[b].reshape(B_W, w)
    kn = _pad_new(kn_ref[rows, :])
    vn = _pad_new(vn_ref[rows, :])
    kt16 = kt.astype(BF16)
    kn16 = kn.astype(BF16)
    es, ens, ls, lses = [], [], [], []
    for gi in range(B_N_GROUPS):
        q_bd = _block_diag_rows(q_refs[gi][rows, :], B_SLOTS).astype(BF16)
        tc = jnp.dot(q_bd, kt16[:, starts[gi]:], preferred_element_type=F32) + bias_refs[gi][...]
        tn = lax.dot_general(q_bd, kn16, NT_DIMS, preferred_element_type=F32) + bias_n_ref[gi]
        m = jnp.maximum(jnp.max(tc, axis=1, keepdims=True), jnp.max(tn, axis=1, keepdims=True))
        ec = jnp.exp2(tc - m)
        en = jnp.exp2(tn - m)
        l = jnp.sum(ec, axis=1, keepdims=True) + jnp.sum(en, axis=1, keepdims=True)
        es.append(ec)
        ens.append(en)
        ls.append(l)
        lses.append(m + jnp.log2(l))
    mx = jnp.maximum(jnp.maximum(lses[0], lses[1]), lses[2])
    ws = [jnp.exp2(x - mx) for x in lses]
    den = ws[0] + ws[1] + ws[2]
    cs = [ws[gi] / (den * ls[gi]) for gi in range(B_N_GROUPS)]
    tiles = []
    for tix in range(w // LANES):
        acc = None
        for gi in range(B_N_GROUPS):
            off = tix * LANES - starts[gi]
            if off >= 0:
                term = cs[gi] * es[gi][:, off:off + LANES]
                acc = term if acc is None else acc + term
        tiles.append(acc)
    pc = jnp.concatenate(tiles, axis=1).astype(BF16)
    pn = (cs[0] * ens[0] + cs[1] * ens[1] + cs[2] * ens[2]).astype(BF16)
    o_bd = (lax.dot_general(pc, vt.astype(BF16), NT_DIMS, preferred_element_type=F32)
            + jnp.dot(pn, vn.astype(BF16), preferred_element_type=F32))
    o_ref[rows, :] = _diag_rows(o_bd, B_SLOTS)
    if ko_ref is not None:
        ko_ref[b] = _shifted(kt, kn.T, s).reshape(ko_ref.shape[1:])
        vo_ref[b] = _shifted(vt, vn.T, s).reshape(vo_ref.shape[1:])


def sample_attn_b(qs, kn, vn, kt, vt, s, emit_cache, bb=1):
    n, slots, d, w = kt.shape
    rows_spec = pl.BlockSpec((bb * s, B_W), lambda i: (i, 0))
    cache = pl.BlockSpec((bb, slots, d, w), lambda i: (i, 0, 0, 0))
    out_specs = [rows_spec]
    out_shape = [jax.ShapeDtypeStruct((n * s, B_W), F32)]
    if emit_cache:
        out_specs += [cache, cache]
        out_shape += [jax.ShapeDtypeStruct((n, slots, d, w), F32)] * 2
    starts = [max(w - (-(-win // LANES)) * LANES, 0) for win, _ in B_PATTERNS]
    scratch = [pltpu.VMEM((slots * s, w - st), F32) for st in starts]
    scratch.append(pltpu.VMEM((B_N_GROUPS, slots * s, LANES), F32))
    return pl.pallas_call(
        functools.partial(_sample_b_kernel, s=s, bb=bb, emit_cache=emit_cache),
        grid=(n // bb,),
        in_specs=[rows_spec] * (B_N_GROUPS + 2) + [cache, cache],
        out_specs=out_specs,
        out_shape=out_shape,
        scratch_shapes=scratch,
        compiler_params=_params(1),
        name="sample_attn_b",
    )(*qs, kn, vn, kt, vt)


def kernel(x_prompt, x_sample, cache_a_k, cache_a_v, cache_b_k, cache_b_v, g_attn, g_ffn, w_qkv_a,
           sinks_a, w_o_a, g_kv, w_kv_s, w_q_b, w_o_b, w_gate_up, w_down, g_final):
    bp, tp, d = x_prompt.shape
    bs, ts, _ = x_sample.shape
    n_a = w_qkv_a.shape[0]
    depth = g_attn.shape[0]
    wa = cache_a_k.shape[2]
    wb = cache_b_k.shape[1]
    wa_p = min(A_WINDOW, tp)
    wb_p = min(max(w for w, _ in B_PATTERNS), tp)
    assert wa == LANES and wb % LANES == 0 and tp % SUPER == 0

    xp = x_prompt
    xs = x_sample.reshape(1, bs * ts, d)

    wq = w_qkv_a[:, :, :A_Q_W].reshape(n_a, d, A_KV_HEADS, A_GROUP, HEAD_DIM)
    wq = wq.transpose(0, 1, 3, 2, 4).reshape(n_a, d, A_Q_W) * QSCALE
    w_qkv = jnp.concatenate([wq, w_qkv_a[:, :, A_Q_W:]], axis=2).astype(BF16)
    w_oa = w_o_a.reshape(n_a, A_KV_HEADS, A_GROUP, HEAD_DIM, d).transpose(0, 2, 1, 3, 4)
    w_oa = w_oa.reshape(n_a, A_Q_W, d).astype(BF16)
    w_kv = w_kv_s.astype(BF16)[None]
    w_qb = (w_q_b * QSCALE).astype(BF16)
    w_ob = w_o_b.astype(BF16)
    w_gu = w_gate_up.astype(BF16)
    w_dn = w_down.astype(BF16)

    cak_t = cache_a_k.transpose(0, 1, 3, 4, 2)
    cav_t = cache_a_v.transpose(0, 1, 3, 4, 2)
    cbk_t = cache_b_k.transpose(0, 2, 3, 1)
    cbv_t = cache_b_v.transpose(0, 2, 3, 1)

    units_a = _units_a()
    a_k_p, a_v_p = [], []
    a_cache = None
    qkv_outs = ((0, A_Q_W, 1, BF16), (A_Q_W, A_KV_W, 1, F32), (A_Q_W + A_KV_W, A_KV_W, 1, F32))
    qkv_outs_s = ((0, A_Q_W, 1, F32),) + qkv_outs[1:]
    qb_outs = tuple((gi * B_W, B_W, r, BF16) for gi, (_, r) in enumerate(B_PATTERNS))
    qb_outs_s = tuple((gi * B_W, B_W, 1, F32) for gi in range(B_N_GROUPS))
    kv_outs = [(0, B_W, -wb_p, F32), (B_W, B_W, -wb_p, F32)]
    for _, r in B_PATTERNS:
        kv_outs += [(0, B_W, r, BF16), (B_W, B_W, r, BF16)]
    kv_outs_s = [(0, B_W, 1, F32), (B_W, B_W, 1, F32)]

    def flat(x):
        return x.reshape(-1, x.shape[-1])

    for l in range(depth):
        last = l == depth - 1
        gfin = g_final if last else None
        if l < n_a:
            qp, kp, vp = norm_mm(xp, [(g_attn[l], w_qkv, l, qkv_outs)], tm=1024)
            qs, ks_, vs_ = norm_mm(xs, [(g_attn[l], w_qkv, l, qkv_outs_s)])
            op = band_attn(qp[:, None], kp[:, None], vp[:, None], sinks_a[l], units_a, 1,
                           A_WINDOW - 1, False, unroll=16)
            osm, ko, vo = sample_attn_a(sinks_a[l], flat(qs), flat(ks_), flat(vs_), cak_t, cav_t, l, ts,
                                        prev=a_cache)
            a_cache = (ko, vo)
            a_k_p.append(kp[:, -wa_p:].reshape(bp, wa_p, A_KV_HEADS, HEAD_DIM))
            a_v_p.append(vp[:, -wa_p:].reshape(bp, wa_p, A_KV_HEADS, HEAD_DIM))
            xp, xs = proj_ffn(flat(xp), flat(op), None, w_oa, l, g_ffn[l], w_gu, w_dn, l, gfin, tm=512,
                              xs=flat(xs), os_=osm)
            xp = xp.reshape(bp, tp, d)
            xs = xs.reshape(1, bs * ts, d)
        else:
            b = l - n_a
            q_proj = (g_attn[l], w_qb, b, qb_outs)
            q_proj_s = (g_attn[l], w_qb, b, qb_outs_s)
            if l == n_a:
                res = norm_mm(xp, [(g_kv, w_kv, 0, kv_outs), q_proj])
                kv, q_groups = res[:len(kv_outs)], res[len(kv_outs):]
                kbp, vbp = kv[0], kv[1]
                k_sub, v_sub = kv[2::2], kv[3::2]
                res = norm_mm(xs, [(g_kv, w_kv, 0, kv_outs_s), q_proj_s])
                kbs, vbs = flat(res[0]), flat(res[1])
                qs_groups = [flat(q) for q in res[2:]]
            else:
                q_groups = norm_mm(xp, [q_proj], tm=1024)
                qs_groups = [flat(q) for q in norm_mm(xs, [q_proj_s])]
            outs, lses = [], []
            for gi, (w, r) in enumerate(B_PATTERNS):
                q_sub = q_groups[gi] if r > 1 else q_groups[gi][:, None]
                k_g = k_sub[gi] if r > 1 else k_sub[gi][:, None]
                v_g = v_sub[gi] if r > 1 else v_sub[gi][:, None]
                o_g, lse_g = band_attn(q_sub, k_g, v_g, None, _units_b(gi), r, w // r, True, unroll=16)
                outs.append(flat(o_g))
                lses.append(lse_g)
            if l == n_a:
                osm, kcb_t, vcb_t = sample_attn_b(qs_groups, kbs, vbs, cbk_t, cbv_t, ts, True)
            else:
                osm = sample_attn_b(qs_groups, kbs, vbs, cbk_t, cbv_t, ts, False, bb=2)[0]
            xp, xs = proj_ffn(flat(xp), outs, lses, w_ob, b, g_ffn[l], w_gu, w_dn, l, gfin, tm=512,
                              xs=flat(xs), os_=osm)
            xp = xp.reshape(bp, tp, d)
            xs = xs.reshape(1, bs * ts, d)

    y_sample = xs.reshape(bs, ts, d)
    b_k_p = kbp.reshape(bp, wb_p, B_SLOTS, HEAD_DIM)
    b_v_p = vbp.reshape(bp, wb_p, B_SLOTS, HEAD_DIM)
    return (xp, y_sample,
            jnp.stack(a_k_p, axis=0), jnp.stack(a_v_p, axis=0), b_k_p, b_v_p,
            a_cache[0].transpose(0, 1, 4, 2, 3), a_cache[1].transpose(0, 1, 4, 2, 3),
            kcb_t.transpose(0, 3, 1, 2), vcb_t.transpose(0, 3, 1, 2))
```
